```python
import jax, jax.numpy as jnp
from jax import lax
import numpy as np

D_MODEL = 1024
BATCH = 8
SEQ = 2048
DEPTH = 1
DEC_BATCH = 128
DEC_SEQ = 4
PAST_LEN = 2048
PAGE_SIZE = 128

N_HEADS = 8
HEAD_DIM = D_MODEL // N_HEADS
N_KV_HEADS = 2
GROUP = N_HEADS // N_KV_HEADS
N_IDX_HEADS = 8
IDX_DIM = 64
TOPK_MAX = 256
Q_BLOCK = 128
D_CONV = D_MODEL
CONV_WIDTH = 31
N_SUBKEYS = 128
N_EXPERTS = N_SUBKEYS * N_SUBKEYS
PEER_HEADS = 8
PEER_KEY_DIM = 256
PEER_HALF = PEER_KEY_DIM // 2
PEER_TOPK = 16
PEER_CHUNK = 256
N_ADA = 6
EPS = 1e-6
NEG_INF = -1e30
SPLITS = (N_HEADS * HEAD_DIM, N_KV_HEADS * HEAD_DIM, N_KV_HEADS * HEAD_DIM,
          N_IDX_HEADS * IDX_DIM, IDX_DIM, N_IDX_HEADS, 2 * D_CONV, 2 * D_MODEL)
D_IN = sum(SPLITS)
F32 = jnp.float32

kernel_name = 'dsa_conformer_peer_adaln_decode_step'


def rms_norm(x, g):
    xf = x.astype(F32)
    y = xf * lax.rsqrt(jnp.mean(xf * xf, axis=-1, keepdims=True) + EPS)
    return (y * g.astype(F32)).astype(x.dtype)


def layer_norm(x, g, b):
    xf = x.astype(F32)
    xc = xf - jnp.mean(xf, axis=-1, keepdims=True)
    y = xc * lax.rsqrt(jnp.mean(xc * xc, axis=-1, keepdims=True) + EPS)
    return (y * g.astype(F32) + b.astype(F32)).astype(x.dtype)


def alibi_slopes():
    return jnp.exp2(-8.0 * jnp.arange(1, N_HEADS + 1, dtype=F32) / N_HEADS)


def ada_mod(c, w_ada, b_ada):
    m = jax.nn.silu(c) @ w_ada + b_ada
    return jnp.split(m[:, None, :], N_ADA, axis=-1)


def take_rows(a, idx):
    return jax.vmap(lambda ab, ib: ab[ib])(a, idx)


def sparse_attn_block(q, qi, wi, q_pos, ki_all, topk, gather_kv, slopes):
    n, tq = q.shape[:2]
    L = ki_all.shape[1]
    s = jnp.einsum('nthd,nsd->nths', qi.astype(F32), ki_all.astype(F32)) * (IDX_DIM ** -0.5)
    iscore = jnp.einsum('nths,nth->nts', jax.nn.relu(s), wi.astype(F32)) * (N_IDX_HEADS ** -0.5)
    causal = jnp.arange(L, dtype=jnp.int32)[None, :] <= q_pos[:, None]
    iscore = jnp.where(causal[None], iscore, NEG_INF)
    _, idx = lax.top_k(iscore, topk)
    valid = idx <= q_pos[None, :, None]
    k_sel, v_sel = gather_kv(idx)
    qg = q.reshape(n, tq, N_KV_HEADS, GROUP, HEAD_DIM).astype(F32)
    logits = jnp.einsum('ntvgd,ntjvd->ntvgj', qg, k_sel.astype(F32)) * (HEAD_DIM ** -0.5)
    dist = (q_pos[None, :, None] - idx).astype(F32)
    logits = logits - slopes.reshape(N_KV_HEADS, GROUP)[None, None, :, :, None] * dist[:, :, None, None, :]
    logits = jnp.where(valid[:, :, None, None, :], logits, NEG_INF)
    p = jax.nn.softmax(logits, axis=-1)
    out = jnp.einsum('ntvgj,ntjvd->ntvgd', p, v_sel.astype(F32))
    return out.reshape(n, tq, N_HEADS * HEAD_DIM).astype(q.dtype)


def prompt_attention(q, k, v, qi, ki, wi):
    b, s = q.shape[:2]
    nblk = s // Q_BLOCK
    topk = min(TOPK_MAX, s // 4)
    slopes = alibi_slopes()

    def gather_kv(idx):
        return take_rows(k, idx), take_rows(v, idx)

    def blocks(a):
        return jnp.swapaxes(a.reshape((b, nblk, Q_BLOCK) + a.shape[2:]), 0, 1)

    pos = jnp.arange(s, dtype=jnp.int32).reshape(nblk, Q_BLOCK)
    out = lax.map(lambda a: sparse_attn_block(a[0], a[1], a[2], a[3], ki, topk, gather_kv, slopes),
                  (blocks(q), blocks(qi), blocks(wi), pos))
    return jnp.swapaxes(out, 0, 1).reshape(b, s, N_HEADS * HEAD_DIM)


def sample_attention(q, k, v, qi, ki, wi, cache_k, cache_v, cache_idx_k, page_table):
    bd, t = q.shape[:2]
    page = cache_k.shape[1]
    past = page_table.shape[1] * page
    ki_past = cache_idx_k[page_table].reshape(bd, past, IDX_DIM)
    ki_all = jnp.concatenate([ki_past, ki.astype(ki_past.dtype)], axis=1)
    topk = min(TOPK_MAX, (past + t) // 4)

    def gather_kv(idx):
        in_past = (idx < past)[..., None, None]
        pidx = jnp.minimum(idx, past - 1)
        phys = jnp.take_along_axis(page_table, (pidx // page).reshape(bd, -1), axis=1).reshape(idx.shape)
        off = pidx % page
        nidx = jnp.clip(idx - past, 0, t - 1)
        return (jnp.where(in_past, cache_k[phys, off], take_rows(k, nidx)),
                jnp.where(in_past, cache_v[phys, off], take_rows(v, nidx)))

    q_pos = past + jnp.arange(t, dtype=jnp.int32)
    return sparse_attn_block(q, qi, wi, q_pos, ki_all, topk, gather_kv, alibi_slopes())


def conv_branch(u, left, w_dw, b_dw, ln_g, ln_b, w_conv_out):
    a, g = jnp.split(u, 2, axis=-1)
    z = a * jax.nn.sigmoid(g)
    zp = jnp.concatenate([left.astype(z.dtype), z], axis=1)
    y = lax.conv_general_dilated(zp, w_dw[:, None, :].astype(z.dtype), (1,), 'VALID',
                                 dimension_numbers=('NWC', 'WIO', 'NWC'),
                                 feature_group_count=D_CONV) + b_dw
    y = jax.nn.silu(layer_norm(y, ln_g, ln_b))
    return y @ w_conv_out, zp[:, zp.shape[1] - (CONV_WIDTH - 1):]


def peer(h, w_q_peer, sub_keys1, sub_keys2, u_emb, v_emb):
    shape = h.shape
    flat = h.reshape(-1, shape[-1])
    m = flat.shape[0]
    nch = -(-m // PEER_CHUNK)
    flat = jnp.pad(flat, ((0, nch * PEER_CHUNK - m), (0, 0))).reshape(nch, PEER_CHUNK, shape[-1])
    n_cand = PEER_TOPK * PEER_TOPK

    def chunk(xc):
        qh = (xc @ w_q_peer).astype(F32).reshape(PEER_CHUNK, PEER_HEADS, 2, PEER_HALF)
        s1 = jnp.einsum('chd,kd->chk', qh[:, :, 0], sub_keys1.astype(F32))
        s2 = jnp.einsum('chd,kd->chk', qh[:, :, 1], sub_keys2.astype(F32))
        v1, i1 = lax.top_k(s1, PEER_TOPK)
        v2, i2 = lax.top_k(s2, PEER_TOPK)
        cand_s = (v1[..., :, None] + v2[..., None, :]).reshape(PEER_CHUNK, PEER_HEADS, n_cand)
        cand_id = (i1[..., :, None] * N_SUBKEYS + i2[..., None, :]).reshape(PEER_CHUNK, PEER_HEADS, n_cand)
        top_s, top_j = lax.top_k(cand_s, PEER_TOPK)
        eid = jnp.take_along_axis(cand_id, top_j, axis=-1)
        gate = jax.nn.softmax(top_s, axis=-1)
        act = jax.nn.gelu(jnp.einsum('chkd,cd->chk', u_emb[eid], xc).astype(F32))
        return jnp.einsum('chk,chkd->cd', (gate * act).astype(v_emb.dtype), v_emb[eid])

    out = lax.map(chunk, flat).reshape(-1, shape[-1])[:m]
    return out.reshape(shape).astype(h.dtype)


def trunk_layer(x, c, attn_fn, conv_left, p):
    shift1, scale1, gate1, shift2, scale2, gate2 = ada_mod(c, p['w_ada'], p['b_ada'])
    n, t, _ = x.shape
    h = rms_norm(x, p['g_norm1']) * (1 + scale1) + shift1
    points = np.cumsum(SPLITS)[:-1].tolist()
    q, k, v, qi, ki, wi, u_conv, gates = jnp.split(h @ p['w_in'], points, axis=-1)
    q = rms_norm(q.reshape(n, t, N_HEADS, HEAD_DIM), p['g_q'])
    k = rms_norm(k.reshape(n, t, N_KV_HEADS, HEAD_DIM), p['g_k'])
    v = v.reshape(n, t, N_KV_HEADS, HEAD_DIM)
    qi = qi.reshape(n, t, N_IDX_HEADS, IDX_DIM)
    attn = attn_fn(q, k, v, qi, ki, wi)
    conv_out, conv_state = conv_branch(u_conv, conv_left, p['w_dw'], p['b_dw'],
                                       p['ln_g'], p['ln_b'], p['w_conv_out'])
    g_attn, g_conv = jnp.split(jax.nn.sigmoid(gates), 2, axis=-1)
    merged = g_attn * (attn @ p['w_o_attn']) + g_conv * conv_out
    x = x + gate1 * (merged @ p['w_out'])
    h2 = rms_norm(x, p['g_norm2']) * (1 + scale2) + shift2
    x = x + gate2 * peer(h2, p['w_q_peer'], p['sub_keys1'], p['sub_keys2'], p['u_emb'], p['v_emb'])
    return x, k, v, ki, conv_state


def setup_inputs(seed: int = 0) -> dict:
    key = jax.random.key(seed)
    ks = jax.random.split(key, 32)
    n_pages = PAST_LEN // PAGE_SIZE
    n_used = DEC_BATCH * n_pages
    n_pool = n_used + (n_used + 3) // 4

    def nrm(k, shape, s=1.0):
        return s * jax.random.normal(k, shape, F32)

    page_table = jax.random.permutation(ks[6], n_pool)[:n_used].reshape(DEC_BATCH, n_pages).astype(jnp.int32)
    return {
        'x_prompt': nrm(ks[0], (BATCH, SEQ, D_MODEL)),
        'x_sample': nrm(ks[1], (DEC_BATCH, DEC_SEQ, D_MODEL)),
        'cache_k': nrm(ks[2], (n_pool, PAGE_SIZE, N_KV_HEADS, HEAD_DIM)),
        'cache_v': nrm(ks[3], (n_pool, PAGE_SIZE, N_KV_HEADS, HEAD_DIM)),
        'cache_idx_k': nrm(ks[4], (n_pool, PAGE_SIZE, IDX_DIM)),
        'state_conv': nrm(ks[5], (DEC_BATCH, CONV_WIDTH - 1, D_CONV), 0.5),
        'page_table': page_table,
        'c_prompt': nrm(ks[7], (BATCH, D_MODEL)),
        'c_sample': nrm(ks[8], (DEC_BATCH, D_MODEL)),
        'w_ada': nrm(ks[9], (D_MODEL, N_ADA * D_MODEL), 0.5 * D_MODEL ** -0.5),
        'b_ada': nrm(ks[10], (N_ADA * D_MODEL,), 0.01),
        'g_norm1': 1.0 + nrm(ks[11], (D_MODEL,), 0.01),
        'g_norm2': 1.0 + nrm(ks[12], (D_MODEL,), 0.01),
        'w_in': nrm(ks[13], (D_MODEL, D_IN), D_MODEL ** -0.5),
        'g_q': 1.0 + nrm(ks[14], (HEAD_DIM,), 0.01),
        'g_k': 1.0 + nrm(ks[15], (HEAD_DIM,), 0.01),
        'w_o_attn': nrm(ks[16], (N_HEADS * HEAD_DIM, D_MODEL), (N_HEADS * HEAD_DIM) ** -0.5),
        'w_dw': nrm(ks[17], (CONV_WIDTH, D_CONV), CONV_WIDTH ** -0.5),
        'b_dw': nrm(ks[18], (D_CONV,), 0.01),
        'ln_g': 1.0 + nrm(ks[19], (D_CONV,), 0.01),
        'ln_b': nrm(ks[20], (D_CONV,), 0.01),
        'w_conv_out': nrm(ks[21], (D_CONV, D_MODEL), D_CONV ** -0.5),
        'w_out': nrm(ks[22], (D_MODEL, D_MODEL), D_MODEL ** -0.5),
        'w_q_peer': nrm(ks[23], (D_MODEL, PEER_HEADS * PEER_KEY_DIM), D_MODEL ** -0.5),
        'sub_keys1': nrm(ks[24], (N_SUBKEYS, PEER_HALF), PEER_HALF ** -0.5),
        'sub_keys2': nrm(ks[25], (N_SUBKEYS, PEER_HALF), PEER_HALF ** -0.5),
        'u_emb': nrm(ks[26], (N_EXPERTS, D_MODEL), D_MODEL ** -0.5),
        'v_emb': nrm(ks[27], (N_EXPERTS, D_MODEL), 0.25),
    }


def reference(x_prompt, x_sample, cache_k, cache_v, cache_idx_k, state_conv, page_table, c_prompt, c_sample,
              w_ada, b_ada, g_norm1, g_norm2, w_in, g_q, g_k, w_o_attn, w_dw, b_dw, ln_g, ln_b,
              w_conv_out, w_out, w_q_peer, sub_keys1, sub_keys2, u_emb, v_emb):
    p = {'w_ada': w_ada, 'b_ada': b_ada, 'g_norm1': g_norm1, 'g_norm2': g_norm2, 'w_in': w_in,
         'g_q': g_q, 'g_k': g_k, 'w_o_attn': w_o_attn, 'w_dw': w_dw, 'b_dw': b_dw, 'ln_g': ln_g,
         'ln_b': ln_b, 'w_conv_out': w_conv_out, 'w_out': w_out, 'w_q_peer': w_q_peer,
         'sub_keys1': sub_keys1, 'sub_keys2': sub_keys2, 'u_emb': u_emb, 'v_emb': v_emb}

    def sample_attn(q, k, v, qi, ki, wi):
        return sample_attention(q, k, v, qi, ki, wi, cache_k, cache_v, cache_idx_k, page_table)

    y_prompt, y_sample = x_prompt, x_sample
    for _ in range(DEPTH):
        prompt_left = jnp.zeros((x_prompt.shape[0], CONV_WIDTH - 1, D_CONV), x_prompt.dtype)
        y_prompt, k_p, v_p, ki_p, conv_p = trunk_layer(y_prompt, c_prompt, prompt_attention, prompt_left, p)
        y_sample, k_s, v_s, ki_s, conv_s = trunk_layer(y_sample, c_sample, sample_attn, state_conv, p)
    return (y_prompt, y_sample, k_p, v_p, ki_p, conv_p, k_s, v_s, ki_s, conv_s)
```

```python
import functools
import math

import jax
import jax.numpy as jnp
import numpy as np
from jax import lax
from jax.experimental import pallas as pl
from jax.experimental.pallas import tpu as pltpu

F32 = jnp.float32
BF16 = jnp.bfloat16
I32 = jnp.int32

N_HEADS = 8
N_KV_HEADS = 2
N_IDX_HEADS = 8
IDX_DIM = 64
TOPK_MAX = 256
CONV_WIDTH = 31
N_SUBKEYS = 128
PEER_HEADS = 8
PEER_TOPK = 16
N_ADA = 6
EPS = 1e-6
MASKED_DIST = 1e33
INT_MIN = -(2 ** 31)
VMEM_LIMIT = 56 * 1024 * 1024
LANES = 128
SUBLANES = 8


def _cparams(sem):
    return pltpu.CompilerParams(dimension_semantics=sem, vmem_limit_bytes=VMEM_LIMIT)


def _nt_dot(a, b):
    return lax.dot_general(a, b, (((1,), (1,)), ((), ())), preferred_element_type=F32)


def _sortable(x):
    b = lax.bitcast_convert_type(x, I32)
    return jnp.where(b < 0, b ^ jnp.int32(0x7FFFFFFF), b)


def _ada_kernel(c_ref, w_ref, b_ref, o_ref):
    c = c_ref[...]
    s = c * jax.nn.sigmoid(c)
    o_ref[...] = jnp.dot(s.astype(BF16), w_ref[...].astype(BF16), preferred_element_type=F32) + b_ref[...]


def _ada(c, w_ada, b_ada):
    n, d = c.shape
    nd = w_ada.shape[1]
    return pl.pallas_call(
        _ada_kernel,
        grid=(nd // d,),
        in_specs=[pl.BlockSpec((n, d), lambda j: (0, 0)),
                  pl.BlockSpec((d, d), lambda j: (0, j)),
                  pl.BlockSpec((1, d), lambda j: (0, j))],
        out_specs=pl.BlockSpec((n, d), lambda j: (0, j)),
        out_shape=jax.ShapeDtypeStruct((n, nd), F32),
        compiler_params=_cparams(("arbitrary",)),
        name="ada",
    )(c, w_ada, b_ada.reshape(1, nd))


def _inproj_kernel(x_ref, shift_ref, scale_ref, g1_ref, gq_ref, gk_ref, w_ref,
                   q_ref, k_ref, kb_ref, v_ref, vb_ref, qi_ref, kiwi_ref, kib_ref, z_ref, ga_ref, gb_ref,
                   *, d, dh, dkv, dqi):
    x = x_ref[...]
    xn = x * lax.rsqrt(jnp.mean(x * x, axis=-1, keepdims=True) + EPS) * g1_ref[...]
    hb = (xn * (1.0 + scale_ref[...]) + shift_ref[...]).astype(BF16)

    def mm(lo, n):
        return jnp.dot(hb, w_ref[:, lo:lo + n], preferred_element_type=F32)

    def head_norm(t, g):
        return t * lax.rsqrt(jnp.mean(t * t, axis=-1, keepdims=True) + EPS) * g

    off = 0
    qf = mm(off, d)
    off += d
    for h in range(d // dh):
        qh = head_norm(qf[:, h * dh:(h + 1) * dh], gq_ref[...]) * (dh ** -0.5)
        q_ref[:, h * dh:(h + 1) * dh] = qh.astype(BF16)
    kf = mm(off, dkv)
    off += dkv
    for h in range(dkv // dh):
        kh = head_norm(kf[:, h * dh:(h + 1) * dh], gk_ref[...])
        k_ref[:, h * dh:(h + 1) * dh] = kh
        kb_ref[:, h * dh:(h + 1) * dh] = kh.astype(BF16)
    vf = mm(off, dkv)
    off += dkv
    v_ref[...] = vf
    vb_ref[...] = vf.astype(BF16)
    qi_ref[...] = (mm(off, dqi) * (IDX_DIM ** -0.5)).astype(BF16)
    off += dqi
    kiwi = mm(off, LANES)
    off += LANES
    lane = lax.broadcasted_iota(I32, kiwi.shape, 1)
    is_wi = (lane >= IDX_DIM) & (lane < IDX_DIM + N_IDX_HEADS)
    kiwi_ref[...] = jnp.where(is_wi, kiwi * (N_IDX_HEADS ** -0.5), kiwi)
    kib_ref[...] = kiwi.astype(BF16)
    a = mm(off, d)
    off += d
    g = mm(off, d)
    off += d
    z_ref[...] = a * jax.nn.sigmoid(g)
    ga_ref[...] = jax.nn.sigmoid(mm(off, d))
    off += d
    gb_ref[...] = jax.nn.sigmoid(mm(off, d))


def _inproj(x, mod3, g1, gq, gk, w1, *, tm, tiles_per_group):
    m, d = x.shape
    r = mod3.shape[1]
    dh = gq.shape[-1]
    dkv = N_KV_HEADS * dh
    dqi = N_IDX_HEADS * IDX_DIM
    nw = w1.shape[1]
    tpg = tiles_per_group
    row = lambda n: pl.BlockSpec((tm, n), lambda i: (i, 0))
    const = lambda s: pl.BlockSpec(s, lambda i: (0,) * len(s))
    mod = lambda c: pl.BlockSpec((None, r, d), lambda i: (i // tpg, 0, c))
    shp = lambda n, t: jax.ShapeDtypeStruct((m, n), t)
    return pl.pallas_call(
        functools.partial(_inproj_kernel, d=d, dh=dh, dkv=dkv, dqi=dqi),
        grid=(m // tm,),
        in_specs=[row(d), mod(0), mod(1), const((1, d)), const((1, dh)), const((1, dh)), const((d, nw))],
        out_specs=[row(d), row(dkv), row(dkv), row(dkv), row(dkv), row(dqi), row(LANES), row(LANES),
                   row(d), row(d), row(d)],
        out_shape=[shp(d, BF16), shp(dkv, F32), shp(dkv, BF16), shp(dkv, F32), shp(dkv, BF16), shp(dqi, BF16),
                   shp(LANES, F32), shp(LANES, BF16), shp(d, F32), shp(d, F32), shp(d, F32)],
        compiler_params=_cparams(("parallel",)),
        name="inproj",
    )(x, mod3, mod3, g1, gq, gk, w1)


def _count_rows(pred):
    rows, tq = pred.shape
    return jnp.sum(jnp.where(pred, 1, 0).astype(I32).reshape(rows // SUBLANES, SUBLANES, tq), axis=0)


def _topk_threshold(key_ref, n_chunks, kc, topk, n_index_bits, thr_ref, j_ref):
    tq = key_ref.shape[1]

    def count(pred_fn):
        def body(c, acc):
            blk = key_ref[pl.ds(pl.multiple_of(c * kc, kc), kc), :]
            row0 = c * kc
            return acc + _count_rows(pred_fn(blk, row0))
        acc = lax.fori_loop(0, n_chunks, body, jnp.zeros((SUBLANES, tq), I32))
        return jnp.sum(acc, axis=0, keepdims=True)

    def value_step(it, thr_u):
        bit = jnp.left_shift(jnp.int32(1), 31 - it)
        cand_s = (thr_u | bit) ^ jnp.int32(INT_MIN)
        cnt = count(lambda blk, row0: blk >= cand_s)
        return jnp.where(cnt >= topk, thr_u | bit, thr_u)

    thr_u = lax.fori_loop(0, 32, value_step, jnp.zeros((1, tq), I32))
    thr = thr_u ^ jnp.int32(INT_MIN)
    thr_ref[...] = thr
    n_gt = count(lambda blk, row0: blk > thr)
    n_ge = count(lambda blk, row0: blk >= jnp.maximum(thr, jnp.int32(INT_MIN + 1)))
    need = topk - n_gt
    j_ref[...] = jnp.where(thr == jnp.int32(INT_MIN), -1, jnp.int32(2 ** n_index_bits))

    @pl.when(jnp.max(n_ge) > topk)
    def _():
        def index_step(it, j):
            cand = j | jnp.left_shift(jnp.int32(1), n_index_bits - 1 - it)

            def pred(blk, row0):
                sidx = row0 + lax.broadcasted_iota(I32, blk.shape, 0)
                return (blk == thr) & (sidx < cand)
            return jnp.where(count(pred) < need, cand, j)
        j = lax.fori_loop(0, n_index_bits, index_step, jnp.zeros((1, tq), I32))
        j_ref[...] = jnp.where(thr == jnp.int32(INT_MIN), -1, j)


def _pattn_kernel(qi_ref, ki_ref, wit_ref, q_ref, k_ref, vt_ref, o_ref,
                  key_ref, dist_ref, lg_ref, p_ref, thr_ref, j_ref, *, tq, dh, topk, n_index_bits, slopes):
    i = pl.program_id(1)
    n_chunks = i + 1
    kc = tq
    t0 = i * tq
    col = lax.broadcasted_iota(I32, (kc, tq), 1)
    row = lax.broadcasted_iota(I32, (kc, tq), 0)

    qis = [qi_ref[:, h * IDX_DIM:(h + 1) * IDX_DIM] for h in range(N_IDX_HEADS)]

    def score_chunk(c, carry):
        r0 = pl.multiple_of(c * kc, kc)
        kic = ki_ref[pl.ds(r0, kc), 0:IDX_DIM]
        acc = jnp.zeros((kc, tq), F32)
        for h in range(N_IDX_HEADS):
            s = _nt_dot(kic, qis[h])
            acc = acc + jnp.maximum(s, 0.0) * wit_ref[h:h + 1, :]
        causal = (row + c * kc) <= (col + t0)
        key_ref[pl.ds(r0, kc), :] = jnp.where(causal, _sortable(acc), jnp.int32(INT_MIN))
        return carry

    lax.fori_loop(0, n_chunks, score_chunk, 0)
    _topk_threshold(key_ref, n_chunks, kc, topk, n_index_bits, thr_ref, j_ref)
    thr = thr_ref[...]
    jj = j_ref[...]

    def dist_chunk(c, carry):
        r0 = pl.multiple_of(c * kc, kc)
        key = key_ref[pl.ds(r0, kc), :]
        sidx = row + c * kc
        sel = (key > thr) | ((key == thr) & (sidx <= jj))
        dist = ((col + t0) - sidx).astype(F32)
        dist_ref[pl.ds(r0, kc), :] = jnp.where(sel, dist, MASKED_DIST)
        return carry

    lax.fori_loop(0, n_chunks, dist_chunk, 0)

    group = N_HEADS // N_KV_HEADS
    for h in range(N_HEADS):
        v = h // group
        slope = slopes[h]
        qh = q_ref[:, h * dh:(h + 1) * dh]

        def logit_chunk(c, m, v=v, slope=slope, qh=qh):
            r0 = pl.multiple_of(c * kc, kc)
            s = _nt_dot(k_ref[pl.ds(r0, kc), v * dh:(v + 1) * dh], qh)
            lg = s - slope * dist_ref[pl.ds(r0, kc), :]
            lg_ref[pl.ds(r0, kc), :] = lg
            return jnp.maximum(m, jnp.max(lg.reshape(kc // SUBLANES, SUBLANES, tq), axis=0))

        m8 = lax.fori_loop(0, n_chunks, logit_chunk, jnp.full((SUBLANES, tq), -jnp.inf, F32))
        m = jnp.max(m8, axis=0, keepdims=True)

        def pv_chunk(c, carry, v=v, m=m):
            acc, l8 = carry
            r0 = pl.multiple_of(c * kc, kc)
            e = jnp.exp(lg_ref[pl.ds(r0, kc), :] - m)
            l8 = l8 + jnp.sum(e.reshape(kc // SUBLANES, SUBLANES, tq), axis=0)
            acc = acc + jnp.dot(vt_ref[c, v * dh:(v + 1) * dh, :], e.astype(BF16),
                                preferred_element_type=F32)
            return acc, l8

        acc, l8 = lax.fori_loop(0, n_chunks, pv_chunk,
                                (jnp.zeros((dh, tq), F32), jnp.zeros((SUBLANES, tq), F32)))
        inv = 1.0 / jnp.sum(l8, axis=0, keepdims=True)
        o_ref[:, h * dh:(h + 1) * dh] = (acc * inv).T.astype(BF16)


def _pattn(qi, kib, wit, q, kb, vt, *, batch, seq, tq, topk):
    m, d = q.shape
    dh = d // N_HEADS
    dkv = N_KV_HEADS * dh
    dqi = qi.shape[1]
    nq = seq // tq
    slopes = tuple(float(2.0 ** (-8.0 * (h + 1) / N_HEADS)) for h in range(N_HEADS))
    n_index_bits = max(1, int(math.ceil(math.log2(seq))))
    qrow = lambda n: pl.BlockSpec((tq, n), lambda b, i: (b * nq + i, 0))
    return pl.pallas_call(
        functools.partial(_pattn_kernel, tq=tq, dh=dh, topk=topk, n_index_bits=n_index_bits, slopes=slopes),
        grid=(batch, nq),
        in_specs=[qrow(dqi),
                  pl.BlockSpec((seq, LANES), lambda b, i: (b, 0)),
                  pl.BlockSpec((N_IDX_HEADS, tq), lambda b, i: (0, b * nq + i)),
                  qrow(d),
                  pl.BlockSpec((seq, dkv), lambda b, i: (b, 0)),
                  pl.BlockSpec((None, nq, dkv, tq), lambda b, i: (b, 0, 0, 0))],
        out_specs=qrow(d),
        out_shape=jax.ShapeDtypeStruct((m, d), BF16),
        scratch_shapes=[pltpu.VMEM((seq, tq), I32), pltpu.VMEM((seq, tq), F32), pltpu.VMEM((seq, tq), F32),
                        pltpu.VMEM((seq, tq), BF16), pltpu.VMEM((1, tq), I32), pltpu.VMEM((1, tq), I32)],
        compiler_params=_cparams(("parallel", "arbitrary")),
        name="pattn",
    )(qi, kib, wit, q, kb, vt)


def _sidx_kernel(pt_ref, qi_ref, w_ref, kin_ref, *rest, n_pages, page, t_new, lp):
    pages = rest[:n_pages]
    key_ref = rest[n_pages]
    kall_ref = rest[n_pages + 1]
    past = n_pages * page
    for p in range(n_pages):
        kall_ref[p * page:(p + 1) * page, :] = pages[p][...].astype(BF16)
    kall_ref[past:lp, :] = jnp.zeros((lp - past, IDX_DIM), BF16)
    kall_ref[past:past + t_new, :] = kin_ref[:, 0:IDX_DIM]
    s = _nt_dot(qi_ref[...], kall_ref[...])
    sw = jnp.maximum(s, 0.0) * w_ref[...]
    isc = jnp.sum(sw.reshape(t_new, N_IDX_HEADS, lp), axis=1)
    sidx = lax.broadcasted_iota(I32, (t_new, lp), 1)
    t = lax.broadcasted_iota(I32, (t_new, lp), 0)
    causal = sidx <= past + t
    key_ref[...] = jnp.where(causal, _sortable(isc), jnp.int32(INT_MIN))


def _sidx(page_table, qi32, w32, kib_new, cache_idx_k, *, lp):
    bd, n_pages = page_table.shape
    page = cache_idx_k.shape[1]
    t_new = kib_new.shape[1]
    rows = qi32.shape[1]
    page_spec = lambda p: pl.BlockSpec((None, page, IDX_DIM), lambda b, pt, p=p: (pt[b, p], 0, 0))
    grid_spec = pltpu.PrefetchScalarGridSpec(
        num_scalar_prefetch=1,
        grid=(bd,),
        in_specs=[pl.BlockSpec((None, rows, IDX_DIM), lambda b, pt: (b, 0, 0)),
                  pl.BlockSpec((None, rows, 1), lambda b, pt: (b, 0, 0)),
                  pl.BlockSpec((None, t_new, LANES), lambda b, pt: (b, 0, 0))]
        + [page_spec(p) for p in range(n_pages)],
        out_specs=pl.BlockSpec((None, t_new, lp), lambda b, pt: (b, 0, 0)),
        scratch_shapes=[pltpu.VMEM((lp, IDX_DIM), BF16)],
    )
    return pl.pallas_call(
        functools.partial(_sidx_kernel, n_pages=n_pages, page=page, t_new=t_new, lp=lp),
        grid_spec=grid_spec,
        out_shape=jax.ShapeDtypeStruct((bd, t_new, lp), I32),
        compiler_params=_cparams(("arbitrary",)),
        name="sidx",
    )(page_table, qi32, w32, kib_new, *([cache_idx_k] * n_pages))


def _sthr_kernel(key_ref, thr_ref, j_ref, *, kc, topk, n_index_bits):
    _topk_threshold(key_ref, key_ref.shape[0] // kc, kc, topk, n_index_bits, thr_ref, j_ref)


def _sthr(keys_t, *, tq, kc, topk):
    lp, nq = keys_t.shape
    n_index_bits = max(1, int(math.ceil(math.log2(lp))))
    return pl.pallas_call(
        functools.partial(_sthr_kernel, kc=kc, topk=topk, n_index_bits=n_index_bits),
        grid=(nq // tq,),
        in_specs=[pl.BlockSpec((lp, tq), lambda i: (0, i))],
        out_specs=[pl.BlockSpec((1, tq), lambda i: (0, i))] * 2,
        out_shape=[jax.ShapeDtypeStruct((1, nq), I32)] * 2,
        compiler_params=_cparams(("parallel",)),
        name="sthr",
    )(keys_t)


def _sattn_kernel(pt_ref, q_ref, key_ref, thr_ref, j_ref, tpos_ref, slope_ref, kn_ref, vn_ref, *rest,
                  n_pages, page, t_new, lp, dh):
    kpages = rest[:n_pages]
    vpages = rest[n_pages:2 * n_pages]
    o_ref = rest[2 * n_pages]
    kall_ref, vall_ref = rest[2 * n_pages + 1:]
    past = n_pages * page
    dkv = N_KV_HEADS * dh
    for p in range(n_pages):
        kall_ref[p * page:(p + 1) * page, :] = kpages[p][...].astype(BF16)
        vall_ref[p * page:(p + 1) * page, :] = vpages[p][...].astype(BF16)
    kall_ref[past:lp, :] = jnp.zeros((lp - past, dkv), BF16)
    vall_ref[past:lp, :] = jnp.zeros((lp - past, dkv), BF16)
    kall_ref[past:past + t_new, :] = kn_ref[...]
    vall_ref[past:past + t_new, :] = vn_ref[...]

    key = key_ref[...]
    thr = thr_ref[...]
    sidx = lax.broadcasted_iota(I32, key.shape, 1)
    sel = (key > thr) | ((key == thr) & (sidx <= j_ref[...]))
    dist = jnp.where(sel, (tpos_ref[...] - sidx).astype(F32), MASKED_DIST)
    for v in range(N_KV_HEADS):
        s = _nt_dot(q_ref[v], kall_ref[:, v * dh:(v + 1) * dh])
        lg = s - slope_ref[v] * dist
        m = jnp.max(lg, axis=-1, keepdims=True)
        e = jnp.exp(lg - m)
        l = jnp.sum(e, axis=-1, keepdims=True)
        o = jnp.dot(e.astype(BF16), vall_ref[:, v * dh:(v + 1) * dh], preferred_element_type=F32)
        o_ref[v] = (o / l).astype(BF16)


def _sattn(page_table, q16, keys16, thr16, j16, tpos16, slope16, kb_new, vb_new, cache_k, cache_v, *, lp):
    bd, n_pages = page_table.shape
    page = cache_k.shape[1]
    dkv = cache_k.shape[2]
    dh = dkv // N_KV_HEADS
    t_new = kb_new.shape[1]
    rows = q16.shape[2]
    per_seq = lambda *s: pl.BlockSpec((None,) + s, lambda b, pt: (b,) + (0,) * len(s))
    page_spec = lambda p: pl.BlockSpec((None, page, dkv), lambda b, pt, p=p: (pt[b, p], 0, 0))
    grid_spec = pltpu.PrefetchScalarGridSpec(
        num_scalar_prefetch=1,
        grid=(bd,),
        in_specs=[per_seq(N_KV_HEADS, rows, dh), per_seq(rows, lp), per_seq(rows, 1), per_seq(rows, 1),
                  pl.BlockSpec((rows, 1), lambda b, pt: (0, 0)),
                  pl.BlockSpec((N_KV_HEADS, rows, 1), lambda b, pt: (0, 0, 0)),
                  per_seq(t_new, dkv), per_seq(t_new, dkv)]
        + [page_spec(p) for p in range(n_pages)] * 2,
        out_specs=per_seq(N_KV_HEADS, rows, dh),
        scratch_shapes=[pltpu.VMEM((lp, dkv), BF16), pltpu.VMEM((lp, dkv), BF16)],
    )
    return pl.pallas_call(
        functools.partial(_sattn_kernel, n_pages=n_pages, page=page, t_new=t_new, lp=lp, dh=dh),
        grid_spec=grid_spec,
        out_shape=jax.ShapeDtypeStruct((bd, N_KV_HEADS, rows, dh), BF16),
        compiler_params=_cparams(("arbitrary",)),
        name="sattn",
    )(page_table, q16, keys16, thr16, j16, tpos16, slope16, kb_new, vb_new,
      *([cache_k] * n_pages), *([cache_v] * n_pages))


def _ln_swish(y, g, b):
    yc = y - jnp.mean(y, axis=-1, keepdims=True)
    yn = yc * lax.rsqrt(jnp.mean(yc * yc, axis=-1, keepdims=True) + EPS) * g + b
    return yn * jax.nn.sigmoid(yn)


def _pconv_kernel(z_ref, halo_ref, w_ref, b_ref, g_ref, beta_ref, o_ref, zs_ref, *, tm, halo):
    first = pl.program_id(1) == 0
    zs_ref[0:halo, :] = jnp.where(first, 0.0, halo_ref[...])
    zs_ref[halo:halo + tm, :] = z_ref[...]
    lead = halo - (CONV_WIDTH - 1)
    acc = jnp.zeros(z_ref.shape, F32) + b_ref[...]
    for j in range(CONV_WIDTH):
        acc = acc + w_ref[j:j + 1, :] * zs_ref[lead + j:lead + j + tm, :]
    o_ref[...] = _ln_swish(acc, g_ref[...], beta_ref[...]).astype(BF16)


def _pconv(z, w_dw, b_dw, ln_g, ln_b, *, batch, seq, tm):
    m, d = z.shape
    halo = 32
    nt = seq // tm
    hb = tm // halo
    const = lambda s: pl.BlockSpec(s, lambda b, i: (0,) * len(s))
    return pl.pallas_call(
        functools.partial(_pconv_kernel, tm=tm, halo=halo),
        grid=(batch, nt),
        in_specs=[pl.BlockSpec((tm, d), lambda b, i: (b * nt + i, 0)),
                  pl.BlockSpec((halo, d), lambda b, i: (jnp.maximum((b * nt + i) * hb - 1, 0), 0)),
                  const((CONV_WIDTH, d)), const((1, d)), const((1, d)), const((1, d))],
        out_specs=pl.BlockSpec((tm, d), lambda b, i: (b * nt + i, 0)),
        out_shape=jax.ShapeDtypeStruct((m, d), BF16),
        scratch_shapes=[pltpu.VMEM((halo + tm, d), F32)],
        compiler_params=_cparams(("parallel", "arbitrary")),
        name="pconv",
    )(z, z, w_dw, b_dw.reshape(1, d), ln_g.reshape(1, d), ln_b.reshape(1, d))


def _sconv_kernel(z_ref, st_ref, w_ref, b_ref, g_ref, beta_ref, o_ref, so_ref, *, t_new):
    left = CONV_WIDTH - 1

    def zp(r):
        return st_ref[:, r, :] if r < left else z_ref[:, r - left, :]

    for t in range(t_new):
        acc = jnp.zeros(zp(0).shape, F32) + b_ref[...]
        for j in range(CONV_WIDTH):
            acc = acc + w_ref[j:j + 1, :] * zp(t + j)
        o_ref[:, t, :] = _ln_swish(acc, g_ref[...], beta_ref[...])
    for r in range(left):
        so_ref[:, r, :] = zp(r + t_new)


def _sconv(z3, state, w_dw, b_dw, ln_g, ln_b, *, bs):
    bd, t_new, d = z3.shape
    left = state.shape[1]
    const = lambda s: pl.BlockSpec(s, lambda i: (0,) * len(s))
    return pl.pallas_call(
        functools.partial(_sconv_kernel, t_new=t_new),
        grid=(bd // bs,),
        in_specs=[pl.BlockSpec((bs, t_new, d), lambda i: (i, 0, 0)),
                  pl.BlockSpec((bs, left, d), lambda i: (i, 0, 0)),
                  const((CONV_WIDTH, d)), const((1, d)), const((1, d)), const((1, d))],
        out_specs=[pl.BlockSpec((bs, t_new, d), lambda i: (i, 0, 0)),
                   pl.BlockSpec((bs, left, d), lambda i: (i, 0, 0))],
        out_shape=[jax.ShapeDtypeStruct((bd, t_new, d), F32), jax.ShapeDtypeStruct((bd, left, d), F32)],
        compiler_params=_cparams(("parallel",)),
        name="sconv",
    )(z3, state, w_dw, b_dw.reshape(1, d), ln_g.reshape(1, d), ln_b.reshape(1, d))


def _mid_kernel(x_ref, attn_ref, cv_ref, ga_ref, gb_ref, gate1_ref, shift2_ref, scale2_ref, g2_ref,
                wo_ref, wc_ref, wout_ref, wq_ref, x1_ref, h2_ref, qp_ref):
    ao = jnp.dot(attn_ref[...], wo_ref[...], preferred_element_type=F32)
    co = jnp.dot(cv_ref[...].astype(BF16), wc_ref[...], preferred_element_type=F32)
    merged = ga_ref[...] * ao + gb_ref[...] * co
    x1 = x_ref[...] + gate1_ref[...] * jnp.dot(merged.astype(BF16), wout_ref[...], preferred_element_type=F32)
    x1_ref[...] = x1
    xn = x1 * lax.rsqrt(jnp.mean(x1 * x1, axis=-1, keepdims=True) + EPS) * g2_ref[...]
    h2 = (xn * (1.0 + scale2_ref[...]) + shift2_ref[...]).astype(BF16)
    h2_ref[...] = h2
    qp_ref[...] = jnp.dot(h2, wq_ref[...], preferred_element_type=F32)


def _mid(x, attn, cv, ga, gb, mod3, g2, wo, wc, wout, wq, *, tm, tiles_per_group):
    m, d = x.shape
    r = mod3.shape[1]
    nq = wq.shape[1]
    tpg = tiles_per_group
    row = lambda n: pl.BlockSpec((tm, n), lambda i: (i, 0))
    const = lambda s: pl.BlockSpec(s, lambda i: (0,) * len(s))
    mod = lambda c: pl.BlockSpec((None, r, d), lambda i: (i // tpg, 0, c))
    return pl.pallas_call(
        _mid_kernel,
        grid=(m // tm,),
        in_specs=[row(d), row(d), row(d), row(d), row(d), mod(2), mod(3), mod(4), const((1, d)),
                  const((d, d)), const((d, d)), const((d, d)), const((d, nq))],
        out_specs=[row(d), row(d), row(nq)],
        out_shape=[jax.ShapeDtypeStruct((m, d), F32), jax.ShapeDtypeStruct((m, d), BF16),
                   jax.ShapeDtypeStruct((m, nq), F32)],
        compiler_params=_cparams(("parallel",)),
        name="mid",
    )(x, attn, cv, ga, gb, mod3, mod3, mod3, g2, wo, wc, wout, wq)


def _extract_top(x_ref, n_rows, count, out_ref):
    tq = x_ref.shape[1]

    def step(r, carry):
        x = x_ref[0:n_rows, :]
        m = jnp.max(jnp.max(x.reshape(n_rows // SUBLANES, SUBLANES, tq), axis=0), axis=0, keepdims=True)
        out_ref[pl.ds(r, 1), :] = m
        x_ref[0:n_rows, :] = jnp.where(x == m, -jnp.inf, x)
        return carry

    lax.fori_loop(0, count, step, 0)


def _gelu_tanh(x):
    return 0.5 * x * (1.0 + jnp.tanh(0.7978845608028654 * (x + 0.044715 * x * x * x)))


def _peer_kernel(h2_ref, qp_ref, sk1_ref, sk2_ref, u_ref, vt_ref, x1_ref, gate2_ref, y_ref,
                 s1_ref, s2_ref, e1_ref, e2_ref, thr_ref, acc_ref, ht_ref, work_ref, top1_ref, top2_ref, cand_ref,
                 topc_ref,
                 *, tm, te, half, pairs):
    e_step = pl.program_id(1)
    nk = N_SUBKEYS

    @pl.when(e_step == 0)
    def _():
        acc_ref[...] = jnp.zeros(acc_ref.shape, F32)
        for h in range(PEER_HEADS):
            base = h * 2 * half
            q1 = qp_ref[:, base:base + half].astype(BF16)
            q2 = qp_ref[:, base + half:base + 2 * half].astype(BF16)
            s1 = _nt_dot(sk1_ref[...], q1)
            s2 = _nt_dot(sk2_ref[...], q2)
            s1_ref[h] = s1
            s2_ref[h] = s2
            work_ref[...] = s1
            _extract_top(work_ref, nk, PEER_TOPK, top1_ref)
            work_ref[...] = s2
            _extract_top(work_ref, nk, PEER_TOPK, top2_ref)
            cand_ref[...] = jnp.full(cand_ref.shape, -jnp.inf, F32)
            for n, (a, b) in enumerate(pairs):
                cand_ref[n:n + 1, :] = top1_ref[a:a + 1, :] + top2_ref[b:b + 1, :]
            _extract_top(cand_ref, cand_ref.shape[0], PEER_TOPK, topc_ref)
            topc = topc_ref[...]
            best = topc[0:1, :]
            norm = jnp.sum(jnp.exp(topc - best), axis=0, keepdims=True)
            thr_ref[h] = jnp.broadcast_to(topc[PEER_TOPK - 1:PEER_TOPK, :], (SUBLANES, tm))
            e1_ref[h] = jnp.exp(s1 - top1_ref[0:1, :])
            e2_ref[h] = jnp.exp(s2 - top2_ref[0:1, :]) / norm

    a_t = _nt_dot(u_ref[...], h2_ref[...])
    act = _gelu_tanh(a_t)
    rows_per_i1 = nk
    for jj in range(te // rows_per_i1):
        i1 = e_step * (te // rows_per_i1) + jj
        gsum = jnp.zeros((rows_per_i1, tm), F32)
        for h in range(PEER_HEADS):
            b1 = s1_ref[h, pl.ds(i1, 1), :]
            w1 = e1_ref[h, pl.ds(i1, 1), :]
            sel = (s2_ref[h] + b1) >= thr_ref[h, 0:1, :]
            gsum = gsum + jnp.where(sel, e2_ref[h] * w1, 0.0)
        lo = jj * rows_per_i1
        ht_ref[lo:lo + rows_per_i1, :] = (gsum * act[lo:lo + rows_per_i1, :]).astype(BF16)
    acc_ref[...] += jnp.dot(vt_ref[...], ht_ref[...], preferred_element_type=F32)

    @pl.when(e_step == pl.num_programs(1) - 1)
    def _():
        y_ref[...] = x1_ref[...] + gate2_ref[...] * acc_ref[...].T


def _peer(h2, qp, sk1, sk2, u, vt, x1, mod3, *, tm, te, tiles_per_group):
    m, d = x1.shape
    r = mod3.shape[1]
    nq = qp.shape[1]
    n_exp = u.shape[0]
    half = sk1.shape[1]
    tpg = tiles_per_group
    pairs = tuple((a, b) for a in range(PEER_TOPK) for b in range(PEER_TOPK) if (a + 1) * (b + 1) <= PEER_TOPK)
    n_cand = -(-len(pairs) // SUBLANES) * SUBLANES
    row = lambda n: pl.BlockSpec((tm, n), lambda i, e: (i, 0))
    const = lambda s: pl.BlockSpec(s, lambda i, e: (0,) * len(s))
    return pl.pallas_call(
        functools.partial(_peer_kernel, tm=tm, te=te, half=half, pairs=pairs),
        grid=(m // tm, n_exp // te),
        in_specs=[row(d), row(nq), const(sk1.shape), const(sk2.shape),
                  pl.BlockSpec((te, d), lambda i, e: (e, 0)),
                  pl.BlockSpec((d, te), lambda i, e: (0, e)),
                  row(d),
                  pl.BlockSpec((None, r, d), lambda i, e: (i // tpg, 0, 5))],
        out_specs=row(d),
        out_shape=jax.ShapeDtypeStruct((m, d), F32),
        scratch_shapes=[pltpu.VMEM((PEER_HEADS, N_SUBKEYS, tm), F32)] * 4
        + [pltpu.VMEM((PEER_HEADS, SUBLANES, tm), F32),
           pltpu.VMEM((d, tm), F32),
           pltpu.VMEM((te, tm), BF16),
           pltpu.VMEM((N_SUBKEYS, tm), F32),
           pltpu.VMEM((PEER_TOPK, tm), F32), pltpu.VMEM((PEER_TOPK, tm), F32),
           pltpu.VMEM((n_cand, tm), F32), pltpu.VMEM((PEER_TOPK, tm), F32)],
        compiler_params=_cparams(("parallel", "arbitrary")),
        name="peer",
    )(h2, qp, sk1, sk2, u, vt, x1, mod3)


def _pack_w_in(w_in, d, dkv, dqi):
    o = 0
    cols = []
    for n in (d, dkv, dkv, dqi):
        cols.append(w_in[:, o:o + n])
        o += n
    kiwi = w_in[:, o:o + IDX_DIM + N_IDX_HEADS]
    o += IDX_DIM + N_IDX_HEADS
    cols.append(jnp.pad(kiwi, ((0, 0), (0, LANES - kiwi.shape[1]))))
    for n in (d, d, d, d):
        cols.append(w_in[:, o:o + n])
        o += n
    return jnp.concatenate(cols, axis=1).astype(BF16)


def kernel(x_prompt, x_sample, cache_k, cache_v, cache_idx_k, state_conv, page_table, c_prompt, c_sample, w_ada, b_ada, g_norm1, g_norm2, w_in, g_q, g_k, w_o_attn, w_dw, b_dw, ln_g, ln_b, w_conv_out, w_out, w_q_peer, sub_keys1, sub_keys2, u_emb, v_emb):
    b, s, d = x_prompt.shape
    bd, t_new, _ = x_sample.shape
    dh = g_q.shape[0]
    dkv = N_KV_HEADS * dh
    dqi = N_IDX_HEADS * IDX_DIM
    n_pages = page_table.shape[1]
    page = cache_k.shape[1]
    past = n_pages * page
    mp, ms = b * s, bd * t_new
    group = N_HEADS // N_KV_HEADS

    tm_p = min(256, s)
    tm_s = min(256, ms)
    tq = min(256, s)
    tm_peer = min(512, ms, mp)
    te = 512

    w1 = _pack_w_in(w_in, d, dkv, dqi)
    wo = w_o_attn.astype(BF16)
    wc = w_conv_out.astype(BF16)
    wout = w_out.astype(BF16)
    wq = w_q_peer.astype(BF16)
    sk1 = sub_keys1.astype(BF16)
    sk2 = sub_keys2.astype(BF16)
    u = u_emb.astype(BF16)
    vt = v_emb.astype(BF16).T
    g1 = g_norm1.reshape(1, d)
    g2 = g_norm2.reshape(1, d)
    gq = g_q.reshape(1, dh)
    gk = g_k.reshape(1, dh)

    mod = _ada(jnp.concatenate([c_prompt, c_sample], axis=0), w_ada, b_ada)
    mod_p = mod[:b].reshape(b, 1, N_ADA * d)
    mod_s = jnp.repeat(mod[b:], t_new, axis=0).reshape(ms // tm_s, tm_s, N_ADA * d)

    xp = x_prompt.reshape(mp, d)
    q, k, kb, v, vb, qi, kiwi, kib, z, ga, gb = _inproj(xp, mod_p, g1, gq, gk, w1, tm=tm_p,
                                                        tiles_per_group=s // tm_p)
    wit = kiwi[:, IDX_DIM:IDX_DIM + N_IDX_HEADS].T
    vbt = jnp.swapaxes(vb.reshape(b, s // tq, tq, dkv), 2, 3)
    attn = _pattn(qi, kib, wit, q, kb, vbt, batch=b, seq=s, tq=tq, topk=min(TOPK_MAX, s // 4))
    cv = _pconv(z, w_dw, b_dw, ln_g, ln_b, batch=b, seq=s, tm=tm_p)
    x1, h2, qp = _mid(xp, attn, cv, ga, gb, mod_p, g2, wo, wc, wout, wq, tm=tm_p, tiles_per_group=s // tm_p)
    y_prompt = _peer(h2, qp, sk1, sk2, u, vt, x1, mod_p, tm=tm_peer, te=te, tiles_per_group=s // tm_peer)
    k_prompt = k.reshape(b, s, N_KV_HEADS, dh)
    v_prompt = v.reshape(b, s, N_KV_HEADS, dh)
    idx_k_prompt = kiwi[:, :IDX_DIM].reshape(b, s, IDX_DIM)
    conv_prompt = z.reshape(b, s, d)[:, s - (CONV_WIDTH - 1):]

    xs = x_sample.reshape(ms, d)
    q, k, kb, v, vb, qi, kiwi, kib, z, ga, gb = _inproj(xs, mod_s, g1, gq, gk, w1, tm=tm_s, tiles_per_group=1)
    lp = -(-(past + t_new) // LANES) * LANES
    qi32 = qi.reshape(bd, t_new * N_IDX_HEADS, IDX_DIM)
    w32 = kiwi[:, IDX_DIM:IDX_DIM + N_IDX_HEADS].reshape(bd, t_new * N_IDX_HEADS, 1)
    keys = _sidx(page_table, qi32, w32, kib.reshape(bd, t_new, LANES), cache_idx_k, lp=lp)
    keys_t = keys.reshape(ms, lp).T
    thr, jj = _sthr(keys_t, tq=min(256, ms), kc=LANES, topk=min(TOPK_MAX, (past + t_new) // 4))
    rep = lambda a: jnp.repeat(a.reshape(bd, t_new, -1), group, axis=1)
    q16 = q.reshape(bd, t_new, N_KV_HEADS, group, dh).transpose(0, 2, 1, 3, 4).reshape(bd, N_KV_HEADS,
                                                                                      t_new * group, dh)
    tpos16 = jnp.repeat(past + jnp.arange(t_new, dtype=I32), group).reshape(t_new * group, 1)
    slopes = jnp.exp2(-8.0 * jnp.arange(1, N_HEADS + 1, dtype=F32) / N_HEADS).reshape(N_KV_HEADS, 1, group)
    slope16 = jnp.broadcast_to(slopes, (N_KV_HEADS, t_new, group)).reshape(N_KV_HEADS, t_new * group, 1)
    o16 = _sattn(page_table, q16, rep(keys), rep(thr.reshape(ms, 1)), rep(jj.reshape(ms, 1)), tpos16, slope16,
                 kb.reshape(bd, t_new, dkv), vb.reshape(bd, t_new, dkv),
                 cache_k.reshape(-1, page, dkv), cache_v.reshape(-1, page, dkv), lp=lp)
    attn = o16.reshape(bd, N_KV_HEADS, t_new, group, dh).transpose(0, 2, 1, 3, 4).reshape(ms, d)
    cv3, conv_sample = _sconv(z.reshape(bd, t_new, d), state_conv, w_dw, b_dw, ln_g, ln_b, bs=min(8, bd))
    x1, h2, qp = _mid(xs, attn, cv3.reshape(ms, d), ga, gb, mod_s, g2, wo, wc, wout, wq, tm=tm_s,
                      tiles_per_group=1)
    mod_s_peer = mod_s.reshape(ms // tm_peer, tm_peer, N_ADA * d) if ms >= tm_peer else mod_s
    y_sample = _peer(h2, qp, sk1, sk2, u, vt, x1, mod_s_peer, tm=min(tm_peer, ms), te=te, tiles_per_group=1)

    return (y_prompt.reshape(b, s, d), y_sample.reshape(bd, t_new, d), k_prompt, v_prompt, idx_k_prompt,
            conv_prompt, k.reshape(bd, t_new, N_KV_HEADS, dh), v.reshape(bd, t_new, N_KV_HEADS, dh),
            kiwi[:, :IDX_DIM].reshape(bd, t_new, IDX_DIM), conv_sample)
```

```python
import functools
import math

import jax
import jax.numpy as jnp
import numpy as np
from jax import lax
from jax.experimental import pallas as pl
from jax.experimental.pallas import tpu as pltpu

F32 = jnp.float32
BF16 = jnp.bfloat16
I32 = jnp.int32

N_HEADS = 8
N_KV_HEADS = 2
N_IDX_HEADS = 8
IDX_DIM = 64
TOPK_MAX = 256
CONV_WIDTH = 31
N_SUBKEYS = 128
PEER_HEADS = 8
PEER_TOPK = 16
N_ADA = 6
EPS = 1e-6
MASKED_DIST = 1e33
INT_MIN = -(2 ** 31)
VMEM_LIMIT = 56 * 1024 * 1024
LANES = 128
SUBLANES = 8


def _cparams(sem):
    return pltpu.CompilerParams(dimension_semantics=sem, vmem_limit_bytes=VMEM_LIMIT)


def _nt_dot(a, b):
    return lax.dot_general(a, b, (((1,), (1,)), ((), ())), preferred_element_type=F32)


def _sortable(x):
    b = lax.bitcast_convert_type(x, I32)
    return jnp.where(b < 0, b ^ jnp.int32(0x7FFFFFFF), b)


def _ada_kernel(c_ref, w_ref, b_ref, o_ref):
    c = c_ref[...]
    s = c * jax.nn.sigmoid(c)
    o_ref[...] = jnp.dot(s.astype(BF16), w_ref[...].astype(BF16), preferred_element_type=F32) + b_ref[...]


def _ada(c, w_ada, b_ada):
    n, d = c.shape
    nd = w_ada.shape[1]
    return pl.pallas_call(
        _ada_kernel,
        grid=(nd // d,),
        in_specs=[pl.BlockSpec((n, d), lambda j: (0, 0)),
                  pl.BlockSpec((d, d), lambda j: (0, j)),
                  pl.BlockSpec((1, d), lambda j: (0, j))],
        out_specs=pl.BlockSpec((n, d), lambda j: (0, j)),
        out_shape=jax.ShapeDtypeStruct((n, nd), F32),
        compiler_params=_cparams(("arbitrary",)),
        name="ada",
    )(c, w_ada, b_ada.reshape(1, nd))


def _inproj_kernel(x_ref, shift_ref, scale_ref, g1_ref, gq_ref, gk_ref, w_ref,
                   q_ref, k_ref, kb_ref, v_ref, vb_ref, qi_ref, kiwi_ref, kib_ref, z_ref, ga_ref, gb_ref,
                   *, d, dh, dkv, dqi):
    x = x_ref[...]
    xn = x * lax.rsqrt(jnp.mean(x * x, axis=-1, keepdims=True) + EPS) * g1_ref[...]
    hb = (xn * (1.0 + scale_ref[...]) + shift_ref[...]).astype(BF16)

    def mm(lo, n):
        return jnp.dot(hb, w_ref[:, lo:lo + n], preferred_element_type=F32)

    def head_norm(t, g):
        return t * lax.rsqrt(jnp.mean(t * t, axis=-1, keepdims=True) + EPS) * g

    off = 0
    qf = mm(off, d)
    off += d
    for h in range(d // dh):
        qh = head_norm(qf[:, h * dh:(h + 1) * dh], gq_ref[...]) * (dh ** -0.5)
        q_ref[:, h * dh:(h + 1) * dh] = qh.astype(BF16)
    kf = mm(off, dkv)
    off += dkv
    for h in range(dkv // dh):
        kh = head_norm(kf[:, h * dh:(h + 1) * dh], gk_ref[...])
        k_ref[:, h * dh:(h + 1) * dh] = kh
        kb_ref[:, h * dh:(h + 1) * dh] = kh.astype(BF16)
    vf = mm(off, dkv)
    off += dkv
    v_ref[...] = vf
    vb_ref[...] = vf.astype(BF16)
    qi_ref[...] = (mm(off, dqi) * (IDX_DIM ** -0.5)).astype(BF16)
    off += dqi
    kiwi = mm(off, LANES)
    off += LANES
    lane = lax.broadcasted_iota(I32, kiwi.shape, 1)
    is_wi = (lane >= IDX_DIM) & (lane < IDX_DIM + N_IDX_HEADS)
    kiwi_ref[...] = jnp.where(is_wi, kiwi * (N_IDX_HEADS ** -0.5), kiwi)
    kib_ref[...] = kiwi.astype(BF16)
    a = mm(off, d)
    off += d
    g = mm(off, d)
    off += d
    z_ref[...] = a * jax.nn.sigmoid(g)
    ga_ref[...] = jax.nn.sigmoid(mm(off, d))
    off += d
    gb_ref[...] = jax.nn.sigmoid(mm(off, d))


def _inproj(x, mod3, g1, gq, gk, w1, *, tm, tiles_per_group):
    m, d = x.shape
    r = mod3.shape[1]
    dh = gq.shape[-1]
    dkv = N_KV_HEADS * dh
    dqi = N_IDX_HEADS * IDX_DIM
    nw = w1.shape[1]
    tpg = tiles_per_group
    row = lambda n: pl.BlockSpec((tm, n), lambda i: (i, 0))
    const = lambda s: pl.BlockSpec(s, lambda i: (0,) * len(s))
    mod = lambda c: pl.BlockSpec((None, r, d), lambda i: (i // tpg, 0, c))
    shp = lambda n, t: jax.ShapeDtypeStruct((m, n), t)
    return pl.pallas_call(
        functools.partial(_inproj_kernel, d=d, dh=dh, dkv=dkv, dqi=dqi),
        grid=(m // tm,),
        in_specs=[row(d), mod(0), mod(1), const((1, d)), const((1, dh)), const((1, dh)), const((d, nw))],
        out_specs=[row(d), row(dkv), row(dkv), row(dkv), row(dkv), row(dqi), row(LANES), row(LANES),
                   row(d), row(d), row(d)],
        out_shape=[shp(d, BF16), shp(dkv, F32), shp(dkv, BF16), shp(dkv, F32), shp(dkv, BF16), shp(dqi, BF16),
                   shp(LANES, F32), shp(LANES, BF16), shp(d, F32), shp(d, F32), shp(d, F32)],
        compiler_params=_cparams(("parallel",)),
        name="inproj",
    )(x, mod3, mod3, g1, gq, gk, w1)


def _count_rows(pred):
    rows, tq = pred.shape
    return jnp.sum(jnp.where(pred, 1, 0).astype(I32).reshape(rows // SUBLANES, SUBLANES, tq), axis=0)


def _topk_threshold(key_ref, n_chunks, kc, topk, n_index_bits, thr_ref, j_ref):
    tq = key_ref.shape[1]

    def count(pred_fn):
        def body(c, acc):
            blk = key_ref[pl.ds(pl.multiple_of(c * kc, kc), kc), :]
            row0 = c * kc
            return acc + _count_rows(pred_fn(blk, row0))
        acc = lax.fori_loop(0, n_chunks, body, jnp.zeros((SUBLANES, tq), I32))
        return jnp.sum(acc, axis=0, keepdims=True)

    def value_step(it, thr_u):
        bit = jnp.left_shift(jnp.int32(1), 31 - it)
        cand_s = (thr_u | bit) ^ jnp.int32(INT_MIN)
        cnt = count(lambda blk, row0: blk >= cand_s)
        return jnp.where(cnt >= topk, thr_u | bit, thr_u)

    thr_u = lax.fori_loop(0, 32, value_step, jnp.zeros((1, tq), I32))
    thr = thr_u ^ jnp.int32(INT_MIN)
    thr_ref[...] = thr
    n_gt = count(lambda blk, row0: blk > thr)
    n_ge = count(lambda blk, row0: blk >= jnp.maximum(thr, jnp.int32(INT_MIN + 1)))
    need = topk - n_gt
    j_ref[...] = jnp.where(thr == jnp.int32(INT_MIN), -1, jnp.int32(2 ** n_index_bits))

    @pl.when(jnp.max(n_ge) > topk)
    def _():
        def index_step(it, j):
            cand = j | jnp.left_shift(jnp.int32(1), n_index_bits - 1 - it)

            def pred(blk, row0):
                sidx = row0 + lax.broadcasted_iota(I32, blk.shape, 0)
                return (blk == thr) & (sidx < cand)
            return jnp.where(count(pred) < need, cand, j)
        j = lax.fori_loop(0, n_index_bits, index_step, jnp.zeros((1, tq), I32))
        j_ref[...] = jnp.where(thr == jnp.int32(INT_MIN), -1, j)


def _pattn_kernel(qi_ref, ki_ref, wit_ref, q_ref, k_ref, vt_ref, o_ref,
                  key_ref, dist_ref, lg_ref, acc_ref, thr_ref, j_ref, *, tq, dh, topk, n_index_bits, slopes):
    i = pl.program_id(1)
    n_chunks = i + 1
    kc = tq
    t0 = i * tq
    col = lax.broadcasted_iota(I32, (kc, tq), 1)
    row = lax.broadcasted_iota(I32, (kc, tq), 0)

    qis = [qi_ref[:, h * IDX_DIM:(h + 1) * IDX_DIM] for h in range(N_IDX_HEADS)]

    def score_chunk(c, carry):
        r0 = pl.multiple_of(c * kc, kc)
        kic = ki_ref[pl.ds(r0, kc), 0:IDX_DIM]
        acc = jnp.zeros((kc, tq), F32)
        for h in range(N_IDX_HEADS):
            s = _nt_dot(kic, qis[h])
            acc = acc + jnp.maximum(s, 0.0) * wit_ref[h:h + 1, :]
        causal = (row + c * kc) <= (col + t0)
        key_ref[pl.ds(r0, kc), :] = jnp.where(causal, _sortable(acc), jnp.int32(INT_MIN))
        return carry

    lax.fori_loop(0, n_chunks, score_chunk, 0)
    _topk_threshold(key_ref, n_chunks, kc, topk, n_index_bits, thr_ref, j_ref)
    thr = thr_ref[...]
    jj = j_ref[...]

    def dist_chunk(c, carry):
        r0 = pl.multiple_of(c * kc, kc)
        key = key_ref[pl.ds(r0, kc), :]
        sidx = row + c * kc
        sel = (key > thr) | ((key == thr) & (sidx <= jj))
        dist = ((col + t0) - sidx).astype(F32)
        dist_ref[pl.ds(r0, kc), :] = jnp.where(sel, dist, MASKED_DIST)
        return carry

    lax.fori_loop(0, n_chunks, dist_chunk, 0)

    group = N_HEADS // N_KV_HEADS
    heads = range(N_HEADS)
    qhs = [q_ref[:, h * dh:(h + 1) * dh] for h in heads]

    def fold8(x, op):
        return op(x.reshape(kc // SUBLANES, SUBLANES, tq), axis=0)

    def logit_chunk(c, ms):
        r0 = pl.multiple_of(c * kc, kc)
        dist = dist_ref[pl.ds(r0, kc), :]
        out = []
        for h in heads:
            v = h // group
            s = _nt_dot(k_ref[pl.ds(r0, kc), v * dh:(v + 1) * dh], qhs[h])
            lg = s - slopes[h] * dist
            lg_ref[h, pl.ds(r0, kc), :] = lg
            out.append(jnp.maximum(ms[h], fold8(lg, jnp.max)))
        return tuple(out)

    m8 = lax.fori_loop(0, n_chunks, logit_chunk,
                       tuple(jnp.full((SUBLANES, tq), -jnp.inf, F32) for _ in heads))
    mx = [jnp.max(m, axis=0, keepdims=True) for m in m8]
    acc_ref[...] = jnp.zeros(acc_ref.shape, F32)

    def pv_chunk(c, ls):
        r0 = pl.multiple_of(c * kc, kc)
        out = []
        for h in heads:
            v = h // group
            e = jnp.exp(lg_ref[h, pl.ds(r0, kc), :] - mx[h])
            out.append(ls[h] + fold8(e, jnp.sum))
            acc_ref[h] += jnp.dot(vt_ref[c, v * dh:(v + 1) * dh, :], e.astype(BF16),
                                  preferred_element_type=F32)
        return tuple(out)

    l8 = lax.fori_loop(0, n_chunks, pv_chunk, tuple(jnp.zeros((SUBLANES, tq), F32) for _ in heads))
    for h in heads:
        inv = 1.0 / jnp.sum(l8[h], axis=0, keepdims=True)
        o_ref[:, h * dh:(h + 1) * dh] = (acc_ref[h] * inv).T.astype(BF16)


def _pattn(qi, kib, wit, q, kb, vt, *, batch, seq, tq, topk):
    m, d = q.shape
    dh = d // N_HEADS
    dkv = N_KV_HEADS * dh
    dqi = qi.shape[1]
    nq = seq // tq
    slopes = tuple(float(2.0 ** (-8.0 * (h + 1) / N_HEADS)) for h in range(N_HEADS))
    n_index_bits = max(1, int(math.ceil(math.log2(seq))))
    qrow = lambda n: pl.BlockSpec((tq, n), lambda b, i: (b * nq + i, 0))
    return pl.pallas_call(
        functools.partial(_pattn_kernel, tq=tq, dh=dh, topk=topk, n_index_bits=n_index_bits, slopes=slopes),
        grid=(batch, nq),
        in_specs=[qrow(dqi),
                  pl.BlockSpec((seq, LANES), lambda b, i: (b, 0)),
                  pl.BlockSpec((N_IDX_HEADS, tq), lambda b, i: (0, b * nq + i)),
                  qrow(d),
                  pl.BlockSpec((seq, dkv), lambda b, i: (b, 0)),
                  pl.BlockSpec((None, nq, dkv, tq), lambda b, i: (b, 0, 0, 0))],
        out_specs=qrow(d),
        out_shape=jax.ShapeDtypeStruct((m, d), BF16),
        scratch_shapes=[pltpu.VMEM((seq, tq), I32), pltpu.VMEM((seq, tq), F32),
                        pltpu.VMEM((N_HEADS, seq, tq), F32), pltpu.VMEM((N_HEADS, dh, tq), F32),
                        pltpu.VMEM((1, tq), I32), pltpu.VMEM((1, tq), I32)],
        compiler_params=_cparams(("parallel", "arbitrary")),
        name="pattn",
    )(qi, kib, wit, q, kb, vt)


def _sidx_kernel(pt_ref, qi_ref, w_ref, kin_ref, *rest, n_pages, page, t_new, lp):
    pages = rest[:n_pages]
    key_ref = rest[n_pages]
    kall_ref = rest[n_pages + 1]
    past = n_pages * page
    for p in range(n_pages):
        kall_ref[:, p * page:(p + 1) * page] = pages[p][...].astype(BF16)
    kall_ref[:, past:lp] = jnp.zeros((IDX_DIM, lp - past), BF16)
    kall_ref[:, past:past + t_new] = kin_ref[...]
    s = jnp.dot(qi_ref[...], kall_ref[...], preferred_element_type=F32)
    sw = jnp.maximum(s, 0.0) * w_ref[...]
    isc = jnp.sum(sw.reshape(t_new, N_IDX_HEADS, lp), axis=1)
    sidx = lax.broadcasted_iota(I32, (t_new, lp), 1)
    t = lax.broadcasted_iota(I32, (t_new, lp), 0)
    causal = sidx <= past + t
    key_ref[...] = jnp.where(causal, _sortable(isc), jnp.int32(INT_MIN))


def _sidx(page_table, qi32, w32, kib_new_t, cache_idx_k_t, *, lp):
    bd, n_pages = page_table.shape
    page = cache_idx_k_t.shape[2]
    t_new = kib_new_t.shape[2]
    rows = qi32.shape[1]
    page_spec = lambda p: pl.BlockSpec((None, IDX_DIM, page), lambda b, pt, p=p: (pt[b, p], 0, 0))
    grid_spec = pltpu.PrefetchScalarGridSpec(
        num_scalar_prefetch=1,
        grid=(bd,),
        in_specs=[pl.BlockSpec((None, rows, IDX_DIM), lambda b, pt: (b, 0, 0)),
                  pl.BlockSpec((None, rows, 1), lambda b, pt: (b, 0, 0)),
                  pl.BlockSpec((None, IDX_DIM, t_new), lambda b, pt: (b, 0, 0))]
        + [page_spec(p) for p in range(n_pages)],
        out_specs=pl.BlockSpec((None, t_new, lp), lambda b, pt: (b, 0, 0)),
        scratch_shapes=[pltpu.VMEM((IDX_DIM, lp), BF16)],
    )
    return pl.pallas_call(
        functools.partial(_sidx_kernel, n_pages=n_pages, page=page, t_new=t_new, lp=lp),
        grid_spec=grid_spec,
        out_shape=jax.ShapeDtypeStruct((bd, t_new, lp), I32),
        compiler_params=_cparams(("arbitrary",)),
        name="sidx",
    )(page_table, qi32, w32, kib_new_t, *([cache_idx_k_t] * n_pages))


def _sthr_kernel(key_ref, thr_ref, j_ref, *, kc, topk, n_index_bits):
    _topk_threshold(key_ref, key_ref.shape[0] // kc, kc, topk, n_index_bits, thr_ref, j_ref)


def _sthr(keys_t, *, tq, kc, topk):
    lp, nq = keys_t.shape
    n_index_bits = max(1, int(math.ceil(math.log2(lp))))
    return pl.pallas_call(
        functools.partial(_sthr_kernel, kc=kc, topk=topk, n_index_bits=n_index_bits),
        grid=(nq // tq,),
        in_specs=[pl.BlockSpec((lp, tq), lambda i: (0, i))],
        out_specs=[pl.BlockSpec((1, tq), lambda i: (0, i))] * 2,
        out_shape=[jax.ShapeDtypeStruct((1, nq), I32)] * 2,
        compiler_params=_cparams(("parallel",)),
        name="sthr",
    )(keys_t)


def _sattn_kernel(pt_ref, q_ref, key_ref, thr_ref, j_ref, tpos_ref, slope_ref, kn_ref, vn_ref, *rest,
                  n_pages, page, t_new, lp, dh):
    kpages = rest[:n_pages]
    vpages = rest[n_pages:2 * n_pages]
    o_ref = rest[2 * n_pages]
    kall_ref, vall_ref = rest[2 * n_pages + 1:]
    past = n_pages * page
    for v in range(N_KV_HEADS):
        for p in range(n_pages):
            kall_ref[v, p * page:(p + 1) * page, :] = kpages[p][pl.ds(v, page, stride=N_KV_HEADS), :].astype(BF16)
            vall_ref[v, p * page:(p + 1) * page, :] = vpages[p][pl.ds(v, page, stride=N_KV_HEADS), :].astype(BF16)
        kall_ref[v, past:lp, :] = jnp.zeros((lp - past, dh), BF16)
        vall_ref[v, past:lp, :] = jnp.zeros((lp - past, dh), BF16)
        kall_ref[v, past:past + t_new, :] = kn_ref[:, v * dh:(v + 1) * dh]
        vall_ref[v, past:past + t_new, :] = vn_ref[:, v * dh:(v + 1) * dh]

    key = key_ref[...]
    thr = thr_ref[...]
    sidx = lax.broadcasted_iota(I32, key.shape, 1)
    sel = (key > thr) | ((key == thr) & (sidx <= j_ref[...]))
    dist = jnp.where(sel, (tpos_ref[...] - sidx).astype(F32), MASKED_DIST)
    for v in range(N_KV_HEADS):
        s = _nt_dot(q_ref[v], kall_ref[v])
        lg = s - slope_ref[v] * dist
        m = jnp.max(lg, axis=-1, keepdims=True)
        e = jnp.exp(lg - m)
        l = jnp.sum(e, axis=-1, keepdims=True)
        o = jnp.dot(e.astype(BF16), vall_ref[v], preferred_element_type=F32)
        o_ref[v] = (o / l).astype(BF16)


def _sattn(page_table, q16, keys16, thr16, j16, tpos16, slope16, kb_new, vb_new, cache_k, cache_v, *, lp):
    bd, n_pages = page_table.shape
    dh = cache_k.shape[2]
    page = cache_k.shape[1] // N_KV_HEADS
    dkv = N_KV_HEADS * dh
    t_new = kb_new.shape[1]
    rows = q16.shape[2]
    per_seq = lambda *s: pl.BlockSpec((None,) + s, lambda b, pt: (b,) + (0,) * len(s))
    page_spec = lambda p: pl.BlockSpec((None, page * N_KV_HEADS, dh), lambda b, pt, p=p: (pt[b, p], 0, 0))
    grid_spec = pltpu.PrefetchScalarGridSpec(
        num_scalar_prefetch=1,
        grid=(bd,),
        in_specs=[per_seq(N_KV_HEADS, rows, dh), per_seq(rows, lp), per_seq(rows, 1), per_seq(rows, 1),
                  pl.BlockSpec((rows, 1), lambda b, pt: (0, 0)),
                  pl.BlockSpec((N_KV_HEADS, rows, 1), lambda b, pt: (0, 0, 0)),
                  per_seq(t_new, dkv), per_seq(t_new, dkv)]
        + [page_spec(p) for p in range(n_pages)] * 2,
        out_specs=per_seq(N_KV_HEADS, rows, dh),
        scratch_shapes=[pltpu.VMEM((N_KV_HEADS, lp, dh), BF16), pltpu.VMEM((N_KV_HEADS, lp, dh), BF16)],
    )
    return pl.pallas_call(
        functools.partial(_sattn_kernel, n_pages=n_pages, page=page, t_new=t_new, lp=lp, dh=dh),
        grid_spec=grid_spec,
        out_shape=jax.ShapeDtypeStruct((bd, N_KV_HEADS, rows, dh), BF16),
        compiler_params=_cparams(("arbitrary",)),
        name="sattn",
    )(page_table, q16, keys16, thr16, j16, tpos16, slope16, kb_new, vb_new,
      *([cache_k] * n_pages), *([cache_v] * n_pages))


def _ln_swish(y, g, b):
    yc = y - jnp.mean(y, axis=-1, keepdims=True)
    yn = yc * lax.rsqrt(jnp.mean(yc * yc, axis=-1, keepdims=True) + EPS) * g + b
    return yn * jax.nn.sigmoid(yn)


def _pconv_kernel(z_ref, halo_ref, w_ref, b_ref, g_ref, beta_ref, o_ref, zs_ref, sh_ref, y_ref, *, tm, halo):
    d = z_ref.shape[1]
    n = halo + tm
    first = pl.program_id(1) == 0
    zs_ref[0:halo, :] = jnp.where(first, 0.0, halo_ref[...])
    zs_ref[halo:n, :] = z_ref[...]
    zs_ref[n:n + SUBLANES, :] = jnp.zeros((SUBLANES, d), F32)
    for b in range(SUBLANES):
        sh_ref[b] = zs_ref[b:b + n, :]
    lead = halo - (CONV_WIDTH - 1)
    rows = 64
    for lc in range(d // LANES):
        ls = slice(lc * LANES, (lc + 1) * LANES)

        def row_chunk(rc, carry, ls=ls):
            r0 = pl.multiple_of(rc * rows, rows)
            acc = jnp.zeros((rows, LANES), F32) + b_ref[:, ls]
            for j in range(CONV_WIDTH):
                off = lead + j
                a = off - off % SUBLANES
                acc = acc + w_ref[j:j + 1, ls] * sh_ref[off % SUBLANES, pl.ds(r0 + a, rows), ls]
            y_ref[pl.ds(r0, rows), ls] = acc
            return carry

        lax.fori_loop(0, tm // rows, row_chunk, 0)
    o_ref[...] = _ln_swish(y_ref[...], g_ref[...], beta_ref[...]).astype(BF16)


def _pconv(z, w_dw, b_dw, ln_g, ln_b, *, batch, seq, tm):
    m, d = z.shape
    halo = 32
    nt = seq // tm
    hb = tm // halo
    const = lambda s: pl.BlockSpec(s, lambda b, i: (0,) * len(s))
    return pl.pallas_call(
        functools.partial(_pconv_kernel, tm=tm, halo=halo),
        grid=(batch, nt),
        in_specs=[pl.BlockSpec((tm, d), lambda b, i: (b * nt + i, 0)),
                  pl.BlockSpec((halo, d), lambda b, i: (jnp.maximum((b * nt + i) * hb - 1, 0), 0)),
                  const((CONV_WIDTH, d)), const((1, d)), const((1, d)), const((1, d))],
        out_specs=pl.BlockSpec((tm, d), lambda b, i: (b * nt + i, 0)),
        out_shape=jax.ShapeDtypeStruct((m, d), BF16),
        scratch_shapes=[pltpu.VMEM((halo + tm + SUBLANES, d), F32), pltpu.VMEM((SUBLANES, halo + tm, d), F32),
                        pltpu.VMEM((tm, d), F32)],
        compiler_params=_cparams(("parallel", "arbitrary")),
        name="pconv",
    )(z, z, w_dw, b_dw.reshape(1, d), ln_g.reshape(1, d), ln_b.reshape(1, d))


def _sconv_kernel(z_ref, st_ref, w_ref, b_ref, g_ref, beta_ref, o_ref, so_ref, *, t_new):
    left = CONV_WIDTH - 1

    def zp(r):
        return st_ref[:, r, :] if r < left else z_ref[:, r - left, :]

    for t in range(t_new):
        acc = jnp.zeros(zp(0).shape, F32) + b_ref[...]
        for j in range(CONV_WIDTH):
            acc = acc + w_ref[j:j + 1, :] * zp(t + j)
        o_ref[:, t, :] = _ln_swish(acc, g_ref[...], beta_ref[...])
    for r in range(left):
        so_ref[:, r, :] = zp(r + t_new)


def _sconv(z3, state, w_dw, b_dw, ln_g, ln_b, *, bs):
    bd, t_new, d = z3.shape
    left = state.shape[1]
    const = lambda s: pl.BlockSpec(s, lambda i: (0,) * len(s))
    return pl.pallas_call(
        functools.partial(_sconv_kernel, t_new=t_new),
        grid=(bd // bs,),
        in_specs=[pl.BlockSpec((bs, t_new, d), lambda i: (i, 0, 0)),
                  pl.BlockSpec((bs, left, d), lambda i: (i, 0, 0)),
                  const((CONV_WIDTH, d)), const((1, d)), const((1, d)), const((1, d))],
        out_specs=[pl.BlockSpec((bs, t_new, d), lambda i: (i, 0, 0)),
                   pl.BlockSpec((bs, left, d), lambda i: (i, 0, 0))],
        out_shape=[jax.ShapeDtypeStruct((bd, t_new, d), F32), jax.ShapeDtypeStruct((bd, left, d), F32)],
        compiler_params=_cparams(("parallel",)),
        name="sconv",
    )(z3, state, w_dw, b_dw.reshape(1, d), ln_g.reshape(1, d), ln_b.reshape(1, d))


def _mid_kernel(x_ref, attn_ref, cv_ref, ga_ref, gb_ref, gate1_ref, shift2_ref, scale2_ref, g2_ref,
                wo_ref, wc_ref, wout_ref, wq_ref, x1_ref, h2_ref, qp_ref):
    ao = jnp.dot(attn_ref[...], wo_ref[...], preferred_element_type=F32)
    co = jnp.dot(cv_ref[...].astype(BF16), wc_ref[...], preferred_element_type=F32)
    merged = ga_ref[...] * ao + gb_ref[...] * co
    x1 = x_ref[...] + gate1_ref[...] * jnp.dot(merged.astype(BF16), wout_ref[...], preferred_element_type=F32)
    x1_ref[...] = x1
    xn = x1 * lax.rsqrt(jnp.mean(x1 * x1, axis=-1, keepdims=True) + EPS) * g2_ref[...]
    h2 = (xn * (1.0 + scale2_ref[...]) + shift2_ref[...]).astype(BF16)
    h2_ref[...] = h2
    qp_ref[...] = jnp.dot(h2, wq_ref[...], preferred_element_type=F32)


def _mid(x, attn, cv, ga, gb, mod3, g2, wo, wc, wout, wq, *, tm, tiles_per_group):
    m, d = x.shape
    r = mod3.shape[1]
    nq = wq.shape[1]
    tpg = tiles_per_group
    row = lambda n: pl.BlockSpec((tm, n), lambda i: (i, 0))
    const = lambda s: pl.BlockSpec(s, lambda i: (0,) * len(s))
    mod = lambda c: pl.BlockSpec((None, r, d), lambda i: (i // tpg, 0, c))
    return pl.pallas_call(
        _mid_kernel,
        grid=(m // tm,),
        in_specs=[row(d), row(d), row(d), row(d), row(d), mod(2), mod(3), mod(4), const((1, d)),
                  const((d, d)), const((d, d)), const((d, d)), const((d, nq))],
        out_specs=[row(d), row(d), row(nq)],
        out_shape=[jax.ShapeDtypeStruct((m, d), F32), jax.ShapeDtypeStruct((m, d), BF16),
                   jax.ShapeDtypeStruct((m, nq), F32)],
        compiler_params=_cparams(("parallel",)),
        name="mid",
    )(x, attn, cv, ga, gb, mod3, mod3, mod3, g2, wo, wc, wout, wq)


def _extract_top(x_ref, n_rows, count, out_ref):
    tq = x_ref.shape[1]

    def step(r, carry):
        x = x_ref[0:n_rows, :]
        m = jnp.max(jnp.max(x.reshape(n_rows // SUBLANES, SUBLANES, tq), axis=0), axis=0, keepdims=True)
        out_ref[pl.ds(r, 1), :] = m
        x_ref[0:n_rows, :] = jnp.where(x == m, -jnp.inf, x)
        return carry

    lax.fori_loop(0, count, step, 0)


def _gelu_tanh(x):
    return (0.5 * x) * (1.0 + jnp.tanh(x * (0.7978845608028654 + 0.035677408136300125 * (x * x))))


def _peer_kernel(h2_ref, qp_ref, sk1_ref, sk2_ref, u_ref, vt_ref, x1_ref, gate2_ref, y_ref,
                 s2_ref, e1_ref, e2_ref, sig_ref, acc_ref, work_ref, top1_ref, top2_ref, cand_ref, topc_ref,
                 *part_refs, tm, te, half, pairs, parts):
    e_step = pl.program_id(1)
    nk = N_SUBKEYS
    at_refs = part_refs[:parts]
    ht_refs = part_refs[parts:]

    @pl.when(e_step == 0)
    def _():
        acc_ref[...] = jnp.zeros(acc_ref.shape, F32)
        for h in range(PEER_HEADS):
            base = h * 2 * half
            q1 = qp_ref[:, base:base + half].astype(BF16)
            q2 = qp_ref[:, base + half:base + 2 * half].astype(BF16)
            s1 = _nt_dot(sk1_ref[...], q1)
            s2 = _nt_dot(sk2_ref[...], q2)
            s2_ref[h] = s2
            work_ref[...] = s1
            _extract_top(work_ref, nk, PEER_TOPK, top1_ref)
            work_ref[...] = s2
            _extract_top(work_ref, nk, PEER_TOPK, top2_ref)
            cand_ref[...] = jnp.full(cand_ref.shape, -jnp.inf, F32)
            for n, (a, b) in enumerate(pairs):
                cand_ref[n:n + 1, :] = top1_ref[a:a + 1, :] + top2_ref[b:b + 1, :]
            _extract_top(cand_ref, cand_ref.shape[0], PEER_TOPK, topc_ref)
            topc = topc_ref[...]
            best = topc[0:1, :]
            norm = jnp.sum(jnp.exp(topc - best), axis=0, keepdims=True)
            thr = topc[PEER_TOPK - 1:PEER_TOPK, :]
            sig = jnp.full((nk, tm), jnp.inf, F32)
            for b in range(PEER_TOPK):
                t2 = top2_ref[b:b + 1, :]
                sig = jnp.where((s1 + t2) >= thr, t2, sig)
            sig_ref[h] = sig
            in_top1 = s1 >= top1_ref[PEER_TOPK - 1:PEER_TOPK, :]
            e1_ref[h] = jnp.where(in_top1, jnp.exp(s1 - top1_ref[0:1, :]), 0.0)
            e2_ref[h] = jnp.exp(s2 - top2_ref[0:1, :]) / norm

    rows = nk // 2
    n_i1 = te // nk
    tp = te // parts

    def pre_act(p):
        at_refs[p][...] = _nt_dot(u_ref[p * tp:(p + 1) * tp, :], h2_ref[...])

    def gate_part(p):
        for jl in range(tp // nk):
            i1 = e_step * n_i1 + p * (tp // nk) + jl
            sg_rows = [sig_ref[h, pl.ds(i1, 1), :] for h in range(PEER_HEADS)]
            w1_rows = [e1_ref[h, pl.ds(i1, 1), :] for h in range(PEER_HEADS)]
            for lc in range(tm // LANES):
                ls = slice(lc * LANES, (lc + 1) * LANES)
                sg = [r[:, ls] for r in sg_rows]
                w1 = [r[:, ls] for r in w1_rows]
                for sc in range(nk // rows):
                    rs = slice(sc * rows, (sc + 1) * rows)
                    g = jnp.zeros((rows, LANES), F32)
                    for h in range(PEER_HEADS):
                        g = g + jnp.where(s2_ref[h, rs, ls] >= sg[h], e2_ref[h, rs, ls] * w1[h], 0.0)
                    r0 = jl * nk + sc * rows
                    ht_refs[p][r0:r0 + rows, ls] = (g * _gelu_tanh(at_refs[p][r0:r0 + rows, ls])).astype(BF16)

    def accumulate(p):
        acc_ref[...] += jnp.dot(vt_ref[:, p * tp:(p + 1) * tp], ht_refs[p][...], preferred_element_type=F32)

    pre_act(0)
    for p in range(parts):
        if p + 1 < parts:
            pre_act(p + 1)
        gate_part(p)
        accumulate(p)

    @pl.when(e_step == pl.num_programs(1) - 1)
    def _():
        y_ref[...] = x1_ref[...] + gate2_ref[...] * acc_ref[...].T


def _peer(h2, qp, sk1, sk2, u, vt, x1, mod3, *, tm, te, tiles_per_group):
    m, d = x1.shape
    r = mod3.shape[1]
    nq = qp.shape[1]
    n_exp = u.shape[0]
    half = sk1.shape[1]
    tpg = tiles_per_group
    pairs = tuple((a, b) for a in range(PEER_TOPK) for b in range(PEER_TOPK) if (a + 1) * (b + 1) <= PEER_TOPK)
    n_cand = -(-len(pairs) // SUBLANES) * SUBLANES
    row = lambda n: pl.BlockSpec((tm, n), lambda i, e: (i, 0))
    const = lambda s: pl.BlockSpec(s, lambda i, e: (0,) * len(s))
    parts = 2
    return pl.pallas_call(
        functools.partial(_peer_kernel, tm=tm, te=te, half=half, pairs=pairs, parts=parts),
        grid=(m // tm, n_exp // te),
        in_specs=[row(d), row(nq), const(sk1.shape), const(sk2.shape),
                  pl.BlockSpec((te, d), lambda i, e: (e, 0)),
                  pl.BlockSpec((d, te), lambda i, e: (0, e)),
                  row(d),
                  pl.BlockSpec((None, r, d), lambda i, e: (i // tpg, 0, 5))],
        out_specs=row(d),
        out_shape=jax.ShapeDtypeStruct((m, d), F32),
        scratch_shapes=[pltpu.VMEM((PEER_HEADS, N_SUBKEYS, tm), F32)] * 4
        + [pltpu.VMEM((d, tm), F32),
           pltpu.VMEM((N_SUBKEYS, tm), F32),
           pltpu.VMEM((PEER_TOPK, tm), F32), pltpu.VMEM((PEER_TOPK, tm), F32),
           pltpu.VMEM((n_cand, tm), F32), pltpu.VMEM((PEER_TOPK, tm), F32)]
        + [pltpu.VMEM((te // parts, tm), F32)] * parts + [pltpu.VMEM((te // parts, tm), BF16)] * parts,
        compiler_params=_cparams(("parallel", "arbitrary")),
        name="peer",
    )(h2, qp, sk1, sk2, u, vt, x1, mod3)


def _pack_w_in(w_in, d, dkv, dqi):
    o = 0
    cols = []
    for n in (d, dkv, dkv, dqi):
        cols.append(w_in[:, o:o + n])
        o += n
    kiwi = w_in[:, o:o + IDX_DIM + N_IDX_HEADS]
    o += IDX_DIM + N_IDX_HEADS
    cols.append(jnp.pad(kiwi, ((0, 0), (0, LANES - kiwi.shape[1]))))
    for n in (d, d, d, d):
        cols.append(w_in[:, o:o + n])
        o += n
    return jnp.concatenate(cols, axis=1).astype(BF16)


def kernel(x_prompt, x_sample, cache_k, cache_v, cache_idx_k, state_conv, page_table, c_prompt, c_sample, w_ada, b_ada, g_norm1, g_norm2, w_in, g_q, g_k, w_o_attn, w_dw, b_dw, ln_g, ln_b, w_conv_out, w_out, w_q_peer, sub_keys1, sub_keys2, u_emb, v_emb):
    b, s, d = x_prompt.shape
    bd, t_new, _ = x_sample.shape
    dh = g_q.shape[0]
    dkv = N_KV_HEADS * dh
    dqi = N_IDX_HEADS * IDX_DIM
    n_pages = page_table.shape[1]
    page = cache_k.shape[1]
    past = n_pages * page
    mp, ms = b * s, bd * t_new
    group = N_HEADS // N_KV_HEADS

    tm_p = min(256, s)
    tm_s = min(256, ms)
    tq = min(256, s)
    tm_peer = min(512, ms, mp)
    te = 1024

    w1 = _pack_w_in(w_in, d, dkv, dqi)
    wo = w_o_attn.astype(BF16)
    wc = w_conv_out.astype(BF16)
    wout = w_out.astype(BF16)
    wq = w_q_peer.astype(BF16)
    sk1 = sub_keys1.astype(BF16)
    sk2 = sub_keys2.astype(BF16)
    u = u_emb.astype(BF16)
    vt = v_emb.astype(BF16).T
    g1 = g_norm1.reshape(1, d)
    g2 = g_norm2.reshape(1, d)
    gq = g_q.reshape(1, dh)
    gk = g_k.reshape(1, dh)

    mod = _ada(jnp.concatenate([c_prompt, c_sample], axis=0), w_ada, b_ada)
    mod_p = mod[:b].reshape(b, 1, N_ADA * d)
    mod_s = jnp.repeat(mod[b:], t_new, axis=0).reshape(ms // tm_s, tm_s, N_ADA * d)

    xp = x_prompt.reshape(mp, d)
    q, k, kb, v, vb, qi, kiwi, kib, z, ga, gb = _inproj(xp, mod_p, g1, gq, gk, w1, tm=tm_p,
                                                        tiles_per_group=s // tm_p)
    wit = kiwi[:, IDX_DIM:IDX_DIM + N_IDX_HEADS].T
    vbt = jnp.swapaxes(vb.reshape(b, s // tq, tq, dkv), 2, 3)
    attn = _pattn(qi, kib, wit, q, kb, vbt, batch=b, seq=s, tq=tq, topk=min(TOPK_MAX, s // 4))
    cv = _pconv(z, w_dw, b_dw, ln_g, ln_b, batch=b, seq=s, tm=tm_p)
    x1, h2, qp = _mid(xp, attn, cv, ga, gb, mod_p, g2, wo, wc, wout, wq, tm=tm_p, tiles_per_group=s // tm_p)
    y_prompt = _peer(h2, qp, sk1, sk2, u, vt, x1, mod_p, tm=tm_peer, te=te, tiles_per_group=s // tm_peer)
    k_prompt = k.reshape(b, s, N_KV_HEADS, dh)
    v_prompt = v.reshape(b, s, N_KV_HEADS, dh)
    idx_k_prompt = kiwi[:, :IDX_DIM].reshape(b, s, IDX_DIM)
    conv_prompt = z.reshape(b, s, d)[:, s - (CONV_WIDTH - 1):]

    xs = x_sample.reshape(ms, d)
    q, k, kb, v, vb, qi, kiwi, kib, z, ga, gb = _inproj(xs, mod_s, g1, gq, gk, w1, tm=tm_s, tiles_per_group=1)
    lp = -(-(past + t_new) // LANES) * LANES
    qi32 = qi.reshape(bd, t_new * N_IDX_HEADS, IDX_DIM)
    w32 = kiwi[:, IDX_DIM:IDX_DIM + N_IDX_HEADS].reshape(bd, t_new * N_IDX_HEADS, 1)
    kib_new_t = jnp.swapaxes(kib[:, :IDX_DIM].reshape(bd, t_new, IDX_DIM), 1, 2)
    keys = _sidx(page_table, qi32, w32, kib_new_t, jnp.swapaxes(cache_idx_k, 1, 2), lp=lp)
    keys_t = keys.reshape(ms, lp).T
    thr, jj = _sthr(keys_t, tq=min(256, ms), kc=LANES, topk=min(TOPK_MAX, (past + t_new) // 4))
    rep = lambda a: jnp.repeat(a.reshape(bd, t_new, -1), group, axis=1)
    q16 = q.reshape(bd, t_new, N_KV_HEADS, group, dh).transpose(0, 2, 1, 3, 4).reshape(bd, N_KV_HEADS,
                                                                                      t_new * group, dh)
    tpos16 = jnp.repeat(past + jnp.arange(t_new, dtype=I32), group).reshape(t_new * group, 1)
    slopes = jnp.exp2(-8.0 * jnp.arange(1, N_HEADS + 1, dtype=F32) / N_HEADS).reshape(N_KV_HEADS, 1, group)
    slope16 = jnp.broadcast_to(slopes, (N_KV_HEADS, t_new, group)).reshape(N_KV_HEADS, t_new * group, 1)
    o16 = _sattn(page_table, q16, rep(keys), rep(thr.reshape(ms, 1)), rep(jj.reshape(ms, 1)), tpos16, slope16,
                 kb.reshape(bd, t_new, dkv), vb.reshape(bd, t_new, dkv),
                 cache_k.reshape(-1, page * N_KV_HEADS, dh), cache_v.reshape(-1, page * N_KV_HEADS, dh), lp=lp)
    attn = o16.reshape(bd, N_KV_HEADS, t_new, group, dh).transpose(0, 2, 1, 3, 4).reshape(ms, d)
    cv3, conv_sample = _sconv(z.reshape(bd, t_new, d), state_conv, w_dw, b_dw, ln_g, ln_b, bs=min(8, bd))
    x1, h2, qp = _mid(xs, attn, cv3.reshape(ms, d), ga, gb, mod_s, g2, wo, wc, wout, wq, tm=tm_s,
                      tiles_per_group=1)
    mod_s_peer = mod_s.reshape(ms // tm_peer, tm_peer, N_ADA * d) if ms >= tm_peer else mod_s
    y_sample = _peer(h2, qp, sk1, sk2, u, vt, x1, mod_s_peer, tm=min(tm_peer, ms), te=te, tiles_per_group=1)

    return (y_prompt.reshape(b, s, d), y_sample.reshape(bd, t_new, d), k_prompt, v_prompt, idx_k_prompt,
            conv_prompt, k.reshape(bd, t_new, N_KV_HEADS, dh), v.reshape(bd, t_new, N_KV_HEADS, dh),
            kiwi[:, :IDX_DIM].reshape(bd, t_new, IDX_DIM), conv_sample)
```

```python
import functools
import math

import jax
import jax.numpy as jnp
import numpy as np
from jax import lax
from jax.experimental import pallas as pl
from jax.experimental.pallas import tpu as pltpu

F32 = jnp.float32
BF16 = jnp.bfloat16
I32 = jnp.int32

N_HEADS = 8
N_KV_HEADS = 2
N_IDX_HEADS = 8
IDX_DIM = 64
TOPK_MAX = 256
CONV_WIDTH = 31
N_SUBKEYS = 128
PEER_HEADS = 8
PEER_TOPK = 16
N_ADA = 6
EPS = 1e-6
MASKED_DIST = 1e33
INT_MIN = -(2 ** 31)
VMEM_LIMIT = 56 * 1024 * 1024
LANES = 128
SUBLANES = 8
BF16_TILE = 16


def _cparams(sem):
    return pltpu.CompilerParams(dimension_semantics=sem, vmem_limit_bytes=VMEM_LIMIT)


def _nt_dot(a, b):
    return lax.dot_general(a, b, (((1,), (1,)), ((), ())), preferred_element_type=F32)


def _mod_spec(mod, tm, d, tiles_per_group, chunk):
    if mod.ndim == 2:
        return pl.BlockSpec((tm, d), lambda i, *_: (i, chunk))
    return pl.BlockSpec((None, mod.shape[1], d), lambda i, *_: (i // tiles_per_group, 0, chunk))


def _sortable(x):
    b = lax.bitcast_convert_type(x, I32)
    return jnp.where(b < 0, b ^ jnp.int32(0x7FFFFFFF), b)


def _ada_kernel(c_ref, w_ref, b_ref, o_ref):
    c = c_ref[...]
    s = c * jax.nn.sigmoid(c)
    o_ref[...] = jnp.dot(s.astype(BF16), w_ref[...].astype(BF16), preferred_element_type=F32) + b_ref[...]


def _ada(c, w_ada, b_ada):
    n, d = c.shape
    nd = w_ada.shape[1]
    return pl.pallas_call(
        _ada_kernel,
        grid=(nd // d,),
        in_specs=[pl.BlockSpec((n, d), lambda j: (0, 0)),
                  pl.BlockSpec((d, d), lambda j: (0, j)),
                  pl.BlockSpec((1, d), lambda j: (0, j))],
        out_specs=pl.BlockSpec((n, d), lambda j: (0, j)),
        out_shape=jax.ShapeDtypeStruct((n, nd), F32),
        compiler_params=_cparams(("arbitrary",)),
        name="ada",
    )(c, w_ada, b_ada.reshape(1, nd))


def _inproj_kernel(x_ref, shift_ref, scale_ref, g1_ref, gq_ref, gk_ref, w_ref,
                   q_ref, k_ref, kb_ref, v_ref, vb_ref, qi_ref, kiwi_ref, kib_ref, z_ref, ga_ref, gb_ref,
                   *, d, dh, dkv, dqi):
    x = x_ref[...]
    xn = x * lax.rsqrt(jnp.mean(x * x, axis=-1, keepdims=True) + EPS) * g1_ref[...]
    hb = (xn * (1.0 + scale_ref[...]) + shift_ref[...]).astype(BF16)

    def mm(lo, n):
        return jnp.dot(hb, w_ref[:, lo:lo + n], preferred_element_type=F32)

    def head_norm(t, g):
        return t * lax.rsqrt(jnp.mean(t * t, axis=-1, keepdims=True) + EPS) * g

    off = 0
    qf = mm(off, d)
    off += d
    for h in range(d // dh):
        qh = head_norm(qf[:, h * dh:(h + 1) * dh], gq_ref[...]) * (dh ** -0.5)
        q_ref[:, h * dh:(h + 1) * dh] = qh.astype(BF16)
    kf = mm(off, dkv)
    off += dkv
    for h in range(dkv // dh):
        kh = head_norm(kf[:, h * dh:(h + 1) * dh], gk_ref[...])
        k_ref[:, h * dh:(h + 1) * dh] = kh
        kb_ref[:, h * dh:(h + 1) * dh] = kh.astype(BF16)
    vf = mm(off, dkv)
    off += dkv
    v_ref[...] = vf
    vb_ref[...] = vf.astype(BF16)
    qi_ref[...] = (mm(off, dqi) * (IDX_DIM ** -0.5)).astype(BF16)
    off += dqi
    kiwi = mm(off, LANES)
    off += LANES
    lane = lax.broadcasted_iota(I32, kiwi.shape, 1)
    is_wi = (lane >= IDX_DIM) & (lane < IDX_DIM + N_IDX_HEADS)
    kiwi_ref[...] = jnp.where(is_wi, kiwi * (N_IDX_HEADS ** -0.5), kiwi)
    kib_ref[...] = kiwi.astype(BF16)
    a = mm(off, d)
    off += d
    g = mm(off, d)
    off += d
    z_ref[...] = a * jax.nn.sigmoid(g)
    ga_ref[...] = jax.nn.sigmoid(mm(off, d))
    off += d
    gb_ref[...] = jax.nn.sigmoid(mm(off, d))


def _inproj(x, mod3, g1, gq, gk, w1, *, tm, tiles_per_group):
    m, d = x.shape
    dh = gq.shape[-1]
    dkv = N_KV_HEADS * dh
    dqi = N_IDX_HEADS * IDX_DIM
    nw = w1.shape[1]
    row = lambda n: pl.BlockSpec((tm, n), lambda i: (i, 0))
    const = lambda s: pl.BlockSpec(s, lambda i: (0,) * len(s))
    mod = lambda c: _mod_spec(mod3, tm, d, tiles_per_group, c)
    shp = lambda n, t: jax.ShapeDtypeStruct((m, n), t)
    return pl.pallas_call(
        functools.partial(_inproj_kernel, d=d, dh=dh, dkv=dkv, dqi=dqi),
        grid=(m // tm,),
        in_specs=[row(d), mod(0), mod(1), const((1, d)), const((1, dh)), const((1, dh)), const((d, nw))],
        out_specs=[row(d), row(dkv), row(dkv), row(dkv), row(dkv), row(dqi), row(LANES), row(LANES),
                   row(d), row(d), row(d)],
        out_shape=[shp(d, BF16), shp(dkv, F32), shp(dkv, BF16), shp(dkv, F32), shp(dkv, BF16), shp(dqi, BF16),
                   shp(LANES, F32), shp(LANES, BF16), shp(d, F32), shp(d, F32), shp(d, F32)],
        compiler_params=_cparams(("parallel",)),
        name="inproj",
    )(x, mod3, mod3, g1, gq, gk, w1)


def _count_rows(pred):
    rows, tq = pred.shape
    return jnp.sum(jnp.where(pred, 1, 0).astype(I32).reshape(rows // SUBLANES, SUBLANES, tq), axis=0)


def _topk_threshold(key_ref, n_chunks, kc, topk, n_index_bits, thr_ref, j_ref):
    tq = key_ref.shape[1]

    def count(pred_fn):
        def body(c, acc):
            blk = key_ref[pl.ds(pl.multiple_of(c * kc, kc), kc), :]
            row0 = c * kc
            return acc + _count_rows(pred_fn(blk, row0))
        acc = lax.fori_loop(0, n_chunks, body, jnp.zeros((SUBLANES, tq), I32))
        return jnp.sum(acc, axis=0, keepdims=True)

    def value_step(it, thr_u):
        bit = jnp.left_shift(jnp.int32(1), 31 - it)
        cand_s = (thr_u | bit) ^ jnp.int32(INT_MIN)
        cnt = count(lambda blk, row0: blk >= cand_s)
        return jnp.where(cnt >= topk, thr_u | bit, thr_u)

    thr_u = lax.fori_loop(0, 32, value_step, jnp.zeros((1, tq), I32))
    thr = thr_u ^ jnp.int32(INT_MIN)
    thr_ref[...] = thr
    n_gt = count(lambda blk, row0: blk > thr)
    n_ge = count(lambda blk, row0: blk >= jnp.maximum(thr, jnp.int32(INT_MIN + 1)))
    need = topk - n_gt
    j_ref[...] = jnp.where(thr == jnp.int32(INT_MIN), -1, jnp.int32(2 ** n_index_bits))

    @pl.when(jnp.max(n_ge) > topk)
    def _():
        def index_step(it, j):
            cand = j | jnp.left_shift(jnp.int32(1), n_index_bits - 1 - it)

            def pred(blk, row0):
                sidx = row0 + lax.broadcasted_iota(I32, blk.shape, 0)
                return (blk == thr) & (sidx < cand)
            return jnp.where(count(pred) < need, cand, j)
        j = lax.fori_loop(0, n_index_bits, index_step, jnp.zeros((1, tq), I32))
        j_ref[...] = jnp.where(thr == jnp.int32(INT_MIN), -1, j)


def _pattn_kernel(qi_ref, ki_ref, wit_ref, q_ref, k_ref, vt_ref, o_ref,
                  key_ref, dist_ref, lg_ref, acc_ref, thr_ref, j_ref, *, tq, dh, topk, n_index_bits, slopes):
    i = pl.program_id(1)
    n_chunks = i + 1
    kc = tq
    t0 = i * tq
    col = lax.broadcasted_iota(I32, (kc, tq), 1)
    row = lax.broadcasted_iota(I32, (kc, tq), 0)

    qis = [qi_ref[:, h * IDX_DIM:(h + 1) * IDX_DIM] for h in range(N_IDX_HEADS)]

    def score_chunk(c, carry):
        r0 = pl.multiple_of(c * kc, kc)
        kic = ki_ref[pl.ds(r0, kc), 0:IDX_DIM]
        acc = jnp.zeros((kc, tq), F32)
        for h in range(N_IDX_HEADS):
            s = _nt_dot(kic, qis[h])
            acc = acc + jnp.maximum(s, 0.0) * wit_ref[h:h + 1, :]
        causal = (row + c * kc) <= (col + t0)
        key_ref[pl.ds(r0, kc), :] = jnp.where(causal, _sortable(acc), jnp.int32(INT_MIN))
        return carry

    lax.fori_loop(0, n_chunks, score_chunk, 0)
    _topk_threshold(key_ref, n_chunks, kc, topk, n_index_bits, thr_ref, j_ref)
    thr = thr_ref[...]
    jj = j_ref[...]

    def dist_chunk(c, carry):
        r0 = pl.multiple_of(c * kc, kc)
        key = key_ref[pl.ds(r0, kc), :]
        sidx = row + c * kc
        sel = (key > thr) | ((key == thr) & (sidx <= jj))
        dist = ((col + t0) - sidx).astype(F32)
        dist_ref[pl.ds(r0, kc), :] = jnp.where(sel, dist, MASKED_DIST)
        return carry

    lax.fori_loop(0, n_chunks, dist_chunk, 0)

    group = N_HEADS // N_KV_HEADS
    heads = range(N_HEADS)
    qhs = [q_ref[:, h * dh:(h + 1) * dh] for h in heads]

    def fold8(x, op):
        return op(x.reshape(kc // SUBLANES, SUBLANES, tq), axis=0)

    def logit_chunk(c, ms):
        r0 = pl.multiple_of(c * kc, kc)
        dist = dist_ref[pl.ds(r0, kc), :]
        out = []
        for h in heads:
            v = h // group
            s = _nt_dot(k_ref[pl.ds(r0, kc), v * dh:(v + 1) * dh], qhs[h])
            lg = s - slopes[h] * dist
            lg_ref[h, pl.ds(r0, kc), :] = lg
            out.append(jnp.maximum(ms[h], fold8(lg, jnp.max)))
        return tuple(out)

    m8 = lax.fori_loop(0, n_chunks, logit_chunk,
                       tuple(jnp.full((SUBLANES, tq), -jnp.inf, F32) for _ in heads))
    mx = [jnp.max(m, axis=0, keepdims=True) for m in m8]
    acc_ref[...] = jnp.zeros(acc_ref.shape, F32)

    def pv_chunk(c, ls):
        r0 = pl.multiple_of(c * kc, kc)
        out = []
        for h in heads:
            v = h // group
            e = jnp.exp(lg_ref[h, pl.ds(r0, kc), :] - mx[h])
            out.append(ls[h] + fold8(e, jnp.sum))
            acc_ref[h] += jnp.dot(vt_ref[c, v * dh:(v + 1) * dh, :], e.astype(BF16),
                                  preferred_element_type=F32)
        return tuple(out)

    l8 = lax.fori_loop(0, n_chunks, pv_chunk, tuple(jnp.zeros((SUBLANES, tq), F32) for _ in heads))
    for h in heads:
        inv = 1.0 / jnp.sum(l8[h], axis=0, keepdims=True)
        o_ref[:, h * dh:(h + 1) * dh] = (acc_ref[h] * inv).T.astype(BF16)


def _pattn(qi, kib, wit, q, kb, vt, *, batch, seq, tq, topk):
    m, d = q.shape
    dh = d // N_HEADS
    dkv = N_KV_HEADS * dh
    dqi = qi.shape[1]
    nq = seq // tq
    slopes = tuple(float(2.0 ** (-8.0 * (h + 1) / N_HEADS)) for h in range(N_HEADS))
    n_index_bits = max(1, int(math.ceil(math.log2(seq))))
    qrow = lambda n: pl.BlockSpec((tq, n), lambda b, i: (b * nq + i, 0))
    return pl.pallas_call(
        functools.partial(_pattn_kernel, tq=tq, dh=dh, topk=topk, n_index_bits=n_index_bits, slopes=slopes),
        grid=(batch, nq),
        in_specs=[qrow(dqi),
                  pl.BlockSpec((seq, LANES), lambda b, i: (b, 0)),
                  pl.BlockSpec((N_IDX_HEADS, tq), lambda b, i: (0, b * nq + i)),
                  qrow(d),
                  pl.BlockSpec((seq, dkv), lambda b, i: (b, 0)),
                  pl.BlockSpec((None, nq, dkv, tq), lambda b, i: (b, 0, 0, 0))],
        out_specs=qrow(d),
        out_shape=jax.ShapeDtypeStruct((m, d), BF16),
        scratch_shapes=[pltpu.VMEM((seq, tq), I32), pltpu.VMEM((seq, tq), F32),
                        pltpu.VMEM((N_HEADS, seq, tq), F32), pltpu.VMEM((N_HEADS, dh, tq), F32),
                        pltpu.VMEM((1, tq), I32), pltpu.VMEM((1, tq), I32)],
        compiler_params=_cparams(("parallel", "arbitrary")),
        name="pattn",
    )(qi, kib, wit, q, kb, vt)


def _sidx_kernel(pt_ref, qi_ref, w_ref, kin_ref, *rest, n_pages, page, t_new, lp):
    pages = rest[:n_pages]
    key_ref = rest[n_pages]
    kall_ref = rest[n_pages + 1]
    past = n_pages * page
    for p in range(n_pages):
        kall_ref[:, p * page:(p + 1) * page] = pages[p][...].astype(BF16)
    kall_ref[:, past:lp] = jnp.zeros((IDX_DIM, lp - past), BF16)
    kall_ref[:, past:past + t_new] = kin_ref[...]
    s = jnp.dot(qi_ref[...], kall_ref[...], preferred_element_type=F32)
    sw = jnp.maximum(s, 0.0) * w_ref[...]
    isc = jnp.sum(sw.reshape(t_new, N_IDX_HEADS, lp), axis=1)
    sidx = lax.broadcasted_iota(I32, (t_new, lp), 1)
    t = lax.broadcasted_iota(I32, (t_new, lp), 0)
    causal = sidx <= past + t
    key_ref[...] = jnp.where(causal, _sortable(isc), jnp.int32(INT_MIN))


def _sidx(page_table, qi32, w32, kib_new_t, cache_idx_k_t, *, lp):
    bd, n_pages = page_table.shape
    page = cache_idx_k_t.shape[2]
    t_new = kib_new_t.shape[2]
    rows = qi32.shape[1]
    page_spec = lambda p: pl.BlockSpec((None, IDX_DIM, page), lambda b, pt, p=p: (pt[b, p], 0, 0))
    grid_spec = pltpu.PrefetchScalarGridSpec(
        num_scalar_prefetch=1,
        grid=(bd,),
        in_specs=[pl.BlockSpec((None, rows, IDX_DIM), lambda b, pt: (b, 0, 0)),
                  pl.BlockSpec((None, rows, 1), lambda b, pt: (b, 0, 0)),
                  pl.BlockSpec((None, IDX_DIM, t_new), lambda b, pt: (b, 0, 0))]
        + [page_spec(p) for p in range(n_pages)],
        out_specs=pl.BlockSpec((None, t_new, lp), lambda b, pt: (b, 0, 0)),
        scratch_shapes=[pltpu.VMEM((IDX_DIM, lp), BF16)],
    )
    return pl.pallas_call(
        functools.partial(_sidx_kernel, n_pages=n_pages, page=page, t_new=t_new, lp=lp),
        grid_spec=grid_spec,
        out_shape=jax.ShapeDtypeStruct((bd, t_new, lp), I32),
        compiler_params=_cparams(("arbitrary",)),
        name="sidx",
    )(page_table, qi32, w32, kib_new_t, *([cache_idx_k_t] * n_pages))


def _sthr_kernel(key_ref, thr_ref, j_ref, *, kc, topk, n_index_bits):
    _topk_threshold(key_ref, key_ref.shape[0] // kc, kc, topk, n_index_bits, thr_ref, j_ref)


def _sthr(keys_t, *, tq, kc, topk):
    lp, nq = keys_t.shape
    n_index_bits = max(1, int(math.ceil(math.log2(lp))))
    return pl.pallas_call(
        functools.partial(_sthr_kernel, kc=kc, topk=topk, n_index_bits=n_index_bits),
        grid=(nq // tq,),
        in_specs=[pl.BlockSpec((lp, tq), lambda i: (0, i))],
        out_specs=[pl.BlockSpec((1, tq), lambda i: (0, i))] * 2,
        out_shape=[jax.ShapeDtypeStruct((1, nq), I32)] * 2,
        compiler_params=_cparams(("parallel",)),
        name="sthr",
    )(keys_t)


def _sattn_kernel(pt_ref, q_ref, key_ref, thr_ref, j_ref, tpos_ref, slope_ref, kn_ref, vn_ref, *rest,
                  n_pages, page, t_new, lp, dh):
    kpages = rest[:n_pages]
    vpages = rest[n_pages:2 * n_pages]
    o_ref = rest[2 * n_pages]
    kall_ref, vall_ref = rest[2 * n_pages + 1:]
    past = n_pages * page
    for v in range(N_KV_HEADS):
        for p in range(n_pages):
            kall_ref[v, p * page:(p + 1) * page, :] = kpages[p][pl.ds(v, page, stride=N_KV_HEADS), :].astype(BF16)
            vall_ref[v, p * page:(p + 1) * page, :] = vpages[p][pl.ds(v, page, stride=N_KV_HEADS), :].astype(BF16)
        kall_ref[v, past:lp, :] = jnp.zeros((lp - past, dh), BF16)
        vall_ref[v, past:lp, :] = jnp.zeros((lp - past, dh), BF16)
        kall_ref[v, past:past + t_new, :] = kn_ref[:, v * dh:(v + 1) * dh]
        vall_ref[v, past:past + t_new, :] = vn_ref[:, v * dh:(v + 1) * dh]

    key = key_ref[...]
    thr = thr_ref[...]
    sidx = lax.broadcasted_iota(I32, key.shape, 1)
    sel = (key > thr) | ((key == thr) & (sidx <= j_ref[...]))
    dist = jnp.where(sel, (tpos_ref[...] - sidx).astype(F32), MASKED_DIST)
    for v in range(N_KV_HEADS):
        s = _nt_dot(q_ref[v], kall_ref[v])
        lg = s - slope_ref[v] * dist
        m = jnp.max(lg, axis=-1, keepdims=True)
        e = jnp.exp(lg - m)
        l = jnp.sum(e, axis=-1, keepdims=True)
        o = jnp.dot(e.astype(BF16), vall_ref[v], preferred_element_type=F32)
        o_ref[v] = (o / l).astype(BF16)


def _sattn(page_table, q16, keys16, thr16, j16, tpos16, slope16, kb_new, vb_new, cache_k, cache_v, *, lp):
    bd, n_pages = page_table.shape
    dh = cache_k.shape[2]
    page = cache_k.shape[1] // N_KV_HEADS
    dkv = N_KV_HEADS * dh
    t_new = kb_new.shape[1]
    rows = q16.shape[2]
    per_seq = lambda *s: pl.BlockSpec((None,) + s, lambda b, pt: (b,) + (0,) * len(s))
    page_spec = lambda p: pl.BlockSpec((None, page * N_KV_HEADS, dh), lambda b, pt, p=p: (pt[b, p], 0, 0))
    grid_spec = pltpu.PrefetchScalarGridSpec(
        num_scalar_prefetch=1,
        grid=(bd,),
        in_specs=[per_seq(N_KV_HEADS, rows, dh), per_seq(rows, lp), per_seq(rows, 1), per_seq(rows, 1),
                  pl.BlockSpec((rows, 1), lambda b, pt: (0, 0)),
                  pl.BlockSpec((N_KV_HEADS, rows, 1), lambda b, pt: (0, 0, 0)),
                  per_seq(t_new, dkv), per_seq(t_new, dkv)]
        + [page_spec(p) for p in range(n_pages)] * 2,
        out_specs=per_seq(N_KV_HEADS, rows, dh),
        scratch_shapes=[pltpu.VMEM((N_KV_HEADS, lp, dh), BF16), pltpu.VMEM((N_KV_HEADS, lp, dh), BF16)],
    )
    return pl.pallas_call(
        functools.partial(_sattn_kernel, n_pages=n_pages, page=page, t_new=t_new, lp=lp, dh=dh),
        grid_spec=grid_spec,
        out_shape=jax.ShapeDtypeStruct((bd, N_KV_HEADS, rows, dh), BF16),
        compiler_params=_cparams(("arbitrary",)),
        name="sattn",
    )(page_table, q16, keys16, thr16, j16, tpos16, slope16, kb_new, vb_new,
      *([cache_k] * n_pages), *([cache_v] * n_pages))


def _ln_swish(y, g, b):
    yc = y - jnp.mean(y, axis=-1, keepdims=True)
    yn = yc * lax.rsqrt(jnp.mean(yc * yc, axis=-1, keepdims=True) + EPS) * g + b
    return yn * jax.nn.sigmoid(yn)


def _pconv_kernel(z_ref, halo_ref, w_ref, b_ref, g_ref, beta_ref, o_ref, zs_ref, sh_ref, y_ref, *, tm, halo):
    d = z_ref.shape[1]
    n = halo + tm
    first = pl.program_id(1) == 0
    zs_ref[0:halo, :] = jnp.where(first, 0.0, halo_ref[...])
    zs_ref[halo:n, :] = z_ref[...]
    zs_ref[n:n + SUBLANES, :] = jnp.zeros((SUBLANES, d), F32)
    for b in range(SUBLANES):
        sh_ref[b] = zs_ref[b:b + n, :]
    lead = halo - (CONV_WIDTH - 1)
    rows = 64
    for lc in range(d // LANES):
        ls = slice(lc * LANES, (lc + 1) * LANES)

        def row_chunk(rc, carry, ls=ls):
            r0 = pl.multiple_of(rc * rows, rows)
            n_acc = 4
            accs = [jnp.zeros((rows, LANES), F32) + b_ref[:, ls]] + [jnp.zeros((rows, LANES), F32)] * (n_acc - 1)
            for j in range(CONV_WIDTH):
                off = lead + j
                a = off - off % SUBLANES
                accs[j % n_acc] = accs[j % n_acc] + (w_ref[j:j + 1, ls]
                                                     * sh_ref[off % SUBLANES, pl.ds(r0 + a, rows), ls])
            y_ref[pl.ds(r0, rows), ls] = (accs[0] + accs[1]) + (accs[2] + accs[3])
            return carry

        lax.fori_loop(0, tm // rows, row_chunk, 0)
    o_ref[...] = _ln_swish(y_ref[...], g_ref[...], beta_ref[...]).astype(BF16)


def _pconv(z, w_dw, b_dw, ln_g, ln_b, *, batch, seq, tm):
    m, d = z.shape
    halo = 32
    nt = seq // tm
    hb = tm // halo
    const = lambda s: pl.BlockSpec(s, lambda b, i: (0,) * len(s))
    return pl.pallas_call(
        functools.partial(_pconv_kernel, tm=tm, halo=halo),
        grid=(batch, nt),
        in_specs=[pl.BlockSpec((tm, d), lambda b, i: (b * nt + i, 0)),
                  pl.BlockSpec((halo, d), lambda b, i: (jnp.maximum((b * nt + i) * hb - 1, 0), 0)),
                  const((CONV_WIDTH, d)), const((1, d)), const((1, d)), const((1, d))],
        out_specs=pl.BlockSpec((tm, d), lambda b, i: (b * nt + i, 0)),
        out_shape=jax.ShapeDtypeStruct((m, d), BF16),
        scratch_shapes=[pltpu.VMEM((halo + tm + SUBLANES, d), F32), pltpu.VMEM((SUBLANES, halo + tm, d), F32),
                        pltpu.VMEM((tm, d), F32)],
        compiler_params=_cparams(("parallel", "arbitrary")),
        name="pconv",
    )(z, z, w_dw, b_dw.reshape(1, d), ln_g.reshape(1, d), ln_b.reshape(1, d))


def _sconv_kernel(z_ref, st_ref, w_ref, b_ref, g_ref, beta_ref, o_ref, so_ref, *, t_new):
    left = CONV_WIDTH - 1

    def zp(r):
        return st_ref[:, r, :] if r < left else z_ref[:, r - left, :]

    for t in range(t_new):
        acc = jnp.zeros(zp(0).shape, F32) + b_ref[...]
        for j in range(CONV_WIDTH):
            acc = acc + w_ref[j:j + 1, :] * zp(t + j)
        o_ref[:, t, :] = _ln_swish(acc, g_ref[...], beta_ref[...])
    for r in range(left):
        so_ref[:, r, :] = zp(r + t_new)


def _sconv(z3, state, w_dw, b_dw, ln_g, ln_b, *, bs):
    bd, t_new, d = z3.shape
    left = state.shape[1]
    const = lambda s: pl.BlockSpec(s, lambda i: (0,) * len(s))
    return pl.pallas_call(
        functools.partial(_sconv_kernel, t_new=t_new),
        grid=(bd // bs,),
        in_specs=[pl.BlockSpec((bs, t_new, d), lambda i: (i, 0, 0)),
                  pl.BlockSpec((bs, left, d), lambda i: (i, 0, 0)),
                  const((CONV_WIDTH, d)), const((1, d)), const((1, d)), const((1, d))],
        out_specs=[pl.BlockSpec((bs, t_new, d), lambda i: (i, 0, 0)),
                   pl.BlockSpec((bs, left, d), lambda i: (i, 0, 0))],
        out_shape=[jax.ShapeDtypeStruct((bd, t_new, d), F32), jax.ShapeDtypeStruct((bd, left, d), F32)],
        compiler_params=_cparams(("parallel",)),
        name="sconv",
    )(z3, state, w_dw, b_dw.reshape(1, d), ln_g.reshape(1, d), ln_b.reshape(1, d))


def _mid_kernel(x_ref, attn_ref, cv_ref, ga_ref, gb_ref, gate1_ref, shift2_ref, scale2_ref, g2_ref,
                wo_ref, wc_ref, wout_ref, wq_ref, x1_ref, h2_ref, qp_ref):
    ao = jnp.dot(attn_ref[...], wo_ref[...], preferred_element_type=F32)
    co = jnp.dot(cv_ref[...].astype(BF16), wc_ref[...], preferred_element_type=F32)
    merged = ga_ref[...] * ao + gb_ref[...] * co
    x1 = x_ref[...] + gate1_ref[...] * jnp.dot(merged.astype(BF16), wout_ref[...], preferred_element_type=F32)
    x1_ref[...] = x1
    xn = x1 * lax.rsqrt(jnp.mean(x1 * x1, axis=-1, keepdims=True) + EPS) * g2_ref[...]
    h2 = (xn * (1.0 + scale2_ref[...]) + shift2_ref[...]).astype(BF16)
    h2_ref[...] = h2
    qp_ref[...] = jnp.dot(h2, wq_ref[...], preferred_element_type=F32)


def _mid(x, attn, cv, ga, gb, mod3, g2, wo, wc, wout, wq, *, tm, tiles_per_group):
    m, d = x.shape
    nq = wq.shape[1]
    row = lambda n: pl.BlockSpec((tm, n), lambda i: (i, 0))
    const = lambda s: pl.BlockSpec(s, lambda i: (0,) * len(s))
    mod = lambda c: _mod_spec(mod3, tm, d, tiles_per_group, c)
    return pl.pallas_call(
        _mid_kernel,
        grid=(m // tm,),
        in_specs=[row(d), row(d), row(d), row(d), row(d), mod(2), mod(3), mod(4), const((1, d)),
                  const((d, d)), const((d, d)), const((d, d)), const((d, nq))],
        out_specs=[row(d), row(d), row(nq)],
        out_shape=[jax.ShapeDtypeStruct((m, d), F32), jax.ShapeDtypeStruct((m, d), BF16),
                   jax.ShapeDtypeStruct((m, nq), F32)],
        compiler_params=_cparams(("parallel",)),
        name="mid",
    )(x, attn, cv, ga, gb, mod3, mod3, mod3, g2, wo, wc, wout, wq)


def _extract_top(xs, count, out_refs):
    def step(r, xs):
        nxt = []
        for x, out_ref in zip(xs, out_refs):
            rows, lanes = x.shape
            m = jnp.max(jnp.max(x.reshape(rows // SUBLANES, SUBLANES, lanes), axis=0), axis=0, keepdims=True)
            out_ref[pl.ds(r, 1), :] = m
            nxt.append(jnp.where(x == m, -jnp.inf, x))
        return tuple(nxt)

    lax.fori_loop(0, count, step, tuple(xs))


def _merge_exchange_pairs(n):
    t = n.bit_length() - 1
    pairs = []
    p = 1 << (t - 1)
    while p > 0:
        q, r, d = 1 << (t - 1), 0, p
        while d > 0:
            pairs += [(i, i + d) for i in range(n - d) if i & p == r]
            d, q, r = q - p, q >> 1, p
        p >>= 1
    return pairs


def _sorted_top_sublane_tiles(x, count):
    n = count
    v = [x[SUBLANES * j:SUBLANES * (j + 1), :] for j in range(n)]
    for i, j in _merge_exchange_pairs(n):
        v[i], v[j] = jnp.maximum(v[i], v[j]), jnp.minimum(v[i], v[j])
    shift = SUBLANES // 2
    while shift >= 1:
        v = [jnp.maximum(v[j], pltpu.roll(v[n - 1 - j], shift, 0)) for j in range(n)]
        k = n // 2
        while k >= 1:
            for i in range(n):
                if i & k == 0:
                    v[i], v[i + k] = jnp.maximum(v[i], v[i + k]), jnp.minimum(v[i], v[i + k])
            k //= 2
        shift //= 2
    return v


def _gelu_tanh(x):
    return (0.5 * x) * (1.0 + jnp.tanh(x * (0.7978845608028654 + 0.035677408136300125 * (x * x))))


def _peer_kernel(h2_ref, qp_ref, sk1_ref, sk2_ref, u_ref, vt_ref, x1_ref, gate2_ref, y_ref,
                 n1_ref, e1_ref, r2_ref, e2_ref, acc_ref, work_ref, cand_ref, topc_ref,
                 *part_refs, tm, te, half, pairs, parts):
    e_step = pl.program_id(1)
    nk = N_SUBKEYS
    at_refs = part_refs[:parts]
    ht_refs = part_refs[parts:]

    @pl.when(e_step == 0)
    def _():
        acc_ref[...] = jnp.zeros(acc_ref.shape, F32)
        for h in range(PEER_HEADS):
            base = h * 2 * half
            q1 = qp_ref[:, base:base + half].astype(BF16)
            q2 = qp_ref[:, base + half:base + 2 * half].astype(BF16)
            work_ref[0] = _nt_dot(sk1_ref[...], q1)
            work_ref[1] = _nt_dot(sk2_ref[...], q2)
            for lc in range(tm // LANES):
                ls = slice(lc * LANES, (lc + 1) * LANES)
                s1 = work_ref[0, :, ls]
                s2 = work_ref[1, :, ls]
                top1 = _sorted_top_sublane_tiles(s1, PEER_TOPK)
                top2 = _sorted_top_sublane_tiles(s2, PEER_TOPK)
                cand_ref[...] = jnp.full(cand_ref.shape, -jnp.inf, F32)
                for n, (a, b) in enumerate(pairs):
                    cand_ref[n:n + 1, :] = top1[a][0:1, :] + top2[b][0:1, :]
                _extract_top((cand_ref[...],), PEER_TOPK, (topc_ref,))
                topc = topc_ref[...]
                best = topc[0:1, :]
                norm = jnp.sum(jnp.exp(topc - best), axis=0, keepdims=True)
                thr = jnp.broadcast_to(topc[PEER_TOPK - 1:PEER_TOPK, :], (SUBLANES, LANES))
                tiles = (nk // SUBLANES, SUBLANES, LANES)
                s1t = s1.reshape(tiles)
                s2t = s2.reshape(tiles)
                n1 = jnp.zeros(tiles, F32)
                r2 = jnp.zeros(tiles, F32)
                for b in range(PEER_TOPK):
                    n1 = jnp.where((s1t + top2[b]) >= thr, float(b + 1), n1)
                    r2 = jnp.where(top2[b] > s2t, float(b + 1), r2)
                n1_ref[h, :, ls] = n1.reshape(nk, LANES)
                r2_ref[h, :, ls] = r2.reshape(nk, LANES).astype(BF16)
                e1 = jnp.where(s1t >= top1[PEER_TOPK - 1], jnp.exp(s1t - top1[0]), 0.0)
                e1_ref[h, :, ls] = e1.reshape(nk, LANES)
                e2 = jnp.exp(s2t - top2[0]) / jnp.broadcast_to(norm, (SUBLANES, LANES))
                e2_ref[h, :, ls] = e2.reshape(nk, LANES).astype(BF16)

    rows = nk // 2
    n_i1 = te // nk
    tp = te // parts

    def pre_act(p):
        at_refs[p][...] = _nt_dot(u_ref[p * tp:(p + 1) * tp, :], h2_ref[...])

    def gate_part(p):
        for jl in range(tp // nk):
            i1 = e_step * n_i1 + p * (tp // nk) + jl
            n1_rows = [n1_ref[h, pl.ds(i1, 1), :] for h in range(PEER_HEADS)]
            w1_rows = [e1_ref[h, pl.ds(i1, 1), :] for h in range(PEER_HEADS)]
            for lc in range(tm // LANES):
                ls = slice(lc * LANES, (lc + 1) * LANES)
                bc = lambda r: jnp.broadcast_to(r[:, ls], (BF16_TILE, LANES)).astype(BF16)[None]
                n1 = [bc(r) for r in n1_rows]
                w1 = [bc(r) for r in w1_rows]
                for sc in range(nk // rows):
                    rs = slice(sc * rows, (sc + 1) * rows)
                    tiles = (rows // BF16_TILE, BF16_TILE, LANES)
                    g = jnp.zeros(tiles, BF16)
                    for h in range(PEER_HEADS):
                        sel = r2_ref[h, rs, ls].reshape(tiles) < n1[h]
                        g = g + jnp.where(sel, e2_ref[h, rs, ls].reshape(tiles) * w1[h], jnp.zeros((), BF16))
                    r0 = jl * nk + sc * rows
                    act = _gelu_tanh(at_refs[p][r0:r0 + rows, ls]).astype(BF16)
                    ht_refs[p][r0:r0 + rows, ls] = g.reshape(rows, LANES) * act

    def accumulate(p):
        acc_ref[...] += jnp.dot(vt_ref[:, p * tp:(p + 1) * tp], ht_refs[p][...], preferred_element_type=F32)

    pre_act(0)
    for p in range(parts):
        if p + 1 < parts:
            pre_act(p + 1)
        gate_part(p)
        accumulate(p)

    @pl.when(e_step == pl.num_programs(1) - 1)
    def _():
        y_ref[...] = x1_ref[...] + gate2_ref[...] * acc_ref[...].T


def _peer(h2, qp, sk1, sk2, u, vt, x1, mod3, *, tm, te, tiles_per_group):
    m, d = x1.shape
    nq = qp.shape[1]
    n_exp = u.shape[0]
    half = sk1.shape[1]
    pairs = tuple((a, b) for a in range(PEER_TOPK) for b in range(PEER_TOPK) if (a + 1) * (b + 1) <= PEER_TOPK)
    n_cand = -(-len(pairs) // SUBLANES) * SUBLANES
    row = lambda n: pl.BlockSpec((tm, n), lambda i, e: (i, 0))
    const = lambda s: pl.BlockSpec(s, lambda i, e: (0,) * len(s))
    parts = 2
    return pl.pallas_call(
        functools.partial(_peer_kernel, tm=tm, te=te, half=half, pairs=pairs, parts=parts),
        grid=(m // tm, n_exp // te),
        in_specs=[row(d), row(nq), const(sk1.shape), const(sk2.shape),
                  pl.BlockSpec((te, d), lambda i, e: (e, 0)),
                  pl.BlockSpec((d, te), lambda i, e: (0, e)),
                  row(d),
                  _mod_spec(mod3, tm, d, tiles_per_group, 5)],
        out_specs=row(d),
        out_shape=jax.ShapeDtypeStruct((m, d), F32),
        scratch_shapes=[pltpu.VMEM((PEER_HEADS, N_SUBKEYS, tm), F32)] * 2
        + [pltpu.VMEM((PEER_HEADS, N_SUBKEYS, tm), BF16)] * 2
        + [pltpu.VMEM((d, tm), F32),
           pltpu.VMEM((2, N_SUBKEYS, tm), F32),
           pltpu.VMEM((n_cand, LANES), F32), pltpu.VMEM((PEER_TOPK, LANES), F32)]
        + [pltpu.VMEM((te // parts, tm), F32)] * parts + [pltpu.VMEM((te // parts, tm), BF16)] * parts,
        compiler_params=_cparams(("parallel", "arbitrary")),
        name="peer",
    )(h2, qp, sk1, sk2, u, vt, x1, mod3)


def _pack_w_in(w_in, d, dkv, dqi):
    o = 0
    cols = []
    for n in (d, dkv, dkv, dqi):
        cols.append(w_in[:, o:o + n])
        o += n
    kiwi = w_in[:, o:o + IDX_DIM + N_IDX_HEADS]
    o += IDX_DIM + N_IDX_HEADS
    cols.append(jnp.pad(kiwi, ((0, 0), (0, LANES - kiwi.shape[1]))))
    for n in (d, d, d, d):
        cols.append(w_in[:, o:o + n])
        o += n
    return jnp.concatenate(cols, axis=1).astype(BF16)


def kernel(x_prompt, x_sample, cache_k, cache_v, cache_idx_k, state_conv, page_table, c_prompt, c_sample, w_ada, b_ada, g_norm1, g_norm2, w_in, g_q, g_k, w_o_attn, w_dw, b_dw, ln_g, ln_b, w_conv_out, w_out, w_q_peer, sub_keys1, sub_keys2, u_emb, v_emb):
    b, s, d = x_prompt.shape
    bd, t_new, _ = x_sample.shape
    dh = g_q.shape[0]
    dkv = N_KV_HEADS * dh
    dqi = N_IDX_HEADS * IDX_DIM
    n_pages = page_table.shape[1]
    page = cache_k.shape[1]
    past = n_pages * page
    mp, ms = b * s, bd * t_new
    group = N_HEADS // N_KV_HEADS

    tm_p = min(256, s)
    tm_s = min(256, ms)
    tq = min(256, s)
    tm_peer = min(512, ms, mp)
    te = 1024

    w1 = _pack_w_in(w_in, d, dkv, dqi)
    wo = w_o_attn.astype(BF16)
    wc = w_conv_out.astype(BF16)
    wout = w_out.astype(BF16)
    wq = w_q_peer.astype(BF16)
    sk1 = sub_keys1.astype(BF16)
    sk2 = sub_keys2.astype(BF16)
    u = u_emb.astype(BF16)
    vt = v_emb.astype(BF16).T
    g1 = g_norm1.reshape(1, d)
    g2 = g_norm2.reshape(1, d)
    gq = g_q.reshape(1, dh)
    gk = g_k.reshape(1, dh)

    mod = _ada(jnp.concatenate([c_prompt, c_sample], axis=0), w_ada, b_ada)
    mod_p = mod[:b].reshape(b, 1, N_ADA * d)
    mod_s = jnp.repeat(mod[b:], t_new, axis=0)

    xp = x_prompt.reshape(mp, d)
    q, k, kb, v, vb, qi, kiwi, kib, z, ga, gb = _inproj(xp, mod_p, g1, gq, gk, w1, tm=tm_p,
                                                        tiles_per_group=s // tm_p)
    wit = kiwi[:, IDX_DIM:IDX_DIM + N_IDX_HEADS].T
    vbt = jnp.swapaxes(vb.reshape(b, s // tq, tq, dkv), 2, 3)
    attn = _pattn(qi, kib, wit, q, kb, vbt, batch=b, seq=s, tq=tq, topk=min(TOPK_MAX, s // 4))
    cv = _pconv(z, w_dw, b_dw, ln_g, ln_b, batch=b, seq=s, tm=tm_p)
    x1, h2, qp = _mid(xp, attn, cv, ga, gb, mod_p, g2, wo, wc, wout, wq, tm=tm_p, tiles_per_group=s // tm_p)
    y_prompt = _peer(h2, qp, sk1, sk2, u, vt, x1, mod_p, tm=tm_peer, te=te, tiles_per_group=s // tm_peer)
    k_prompt = k.reshape(b, s, N_KV_HEADS, dh)
    v_prompt = v.reshape(b, s, N_KV_HEADS, dh)
    idx_k_prompt = kiwi[:, :IDX_DIM].reshape(b, s, IDX_DIM)
    conv_prompt = z.reshape(b, s, d)[:, s - (CONV_WIDTH - 1):]

    xs = x_sample.reshape(ms, d)
    q, k, kb, v, vb, qi, kiwi, kib, z, ga, gb = _inproj(xs, mod_s, g1, gq, gk, w1, tm=tm_s, tiles_per_group=1)
    lp = -(-(past + t_new) // LANES) * LANES
    qi32 = qi.reshape(bd, t_new * N_IDX_HEADS, IDX_DIM)
    w32 = kiwi[:, IDX_DIM:IDX_DIM + N_IDX_HEADS].reshape(bd, t_new * N_IDX_HEADS, 1)
    kib_new_t = jnp.swapaxes(kib[:, :IDX_DIM].reshape(bd, t_new, IDX_DIM), 1, 2)
    keys = _sidx(page_table, qi32, w32, kib_new_t, jnp.swapaxes(cache_idx_k, 1, 2), lp=lp)
    keys_t = keys.reshape(ms, lp).T
    thr, jj = _sthr(keys_t, tq=min(256, ms), kc=LANES, topk=min(TOPK_MAX, (past + t_new) // 4))
    rep = lambda a: jnp.repeat(a.reshape(bd, t_new, -1), group, axis=1)
    q16 = q.reshape(bd, t_new, N_KV_HEADS, group, dh).transpose(0, 2, 1, 3, 4).reshape(bd, N_KV_HEADS,
                                                                                      t_new * group, dh)
    tpos16 = jnp.repeat(past + jnp.arange(t_new, dtype=I32), group).reshape(t_new * group, 1)
    slopes = jnp.exp2(-8.0 * jnp.arange(1, N_HEADS + 1, dtype=F32) / N_HEADS).reshape(N_KV_HEADS, 1, group)
    slope16 = jnp.broadcast_to(slopes, (N_KV_HEADS, t_new, group)).reshape(N_KV_HEADS, t_new * group, 1)
    o16 = _sattn(page_table, q16, rep(keys), rep(thr.reshape(ms, 1)), rep(jj.reshape(ms, 1)), tpos16, slope16,
                 kb.reshape(bd, t_new, dkv), vb.reshape(bd, t_new, dkv),
                 cache_k.reshape(-1, page * N_KV_HEADS, dh), cache_v.reshape(-1, page * N_KV_HEADS, dh), lp=lp)
    attn = o16.reshape(bd, N_KV_HEADS, t_new, group, dh).transpose(0, 2, 1, 3, 4).reshape(ms, d)
    cv3, conv_sample = _sconv(z.reshape(bd, t_new, d), state_conv, w_dw, b_dw, ln_g, ln_b, bs=min(8, bd))
    x1, h2, qp = _mid(xs, attn, cv3.reshape(ms, d), ga, gb, mod_s, g2, wo, wc, wout, wq, tm=tm_s,
                      tiles_per_group=1)
    y_sample = _peer(h2, qp, sk1, sk2, u, vt, x1, mod_s, tm=min(tm_peer, ms), te=te, tiles_per_group=1)

    return (y_prompt.reshape(b, s, d), y_sample.reshape(bd, t_new, d), k_prompt, v_prompt, idx_k_prompt,
            conv_prompt, k.reshape(bd, t_new, N_KV_HEADS, dh), v.reshape(bd, t_new, N_KV_HEADS, dh),
            kiwi[:, :IDX_DIM].reshape(bd, t_new, IDX_DIM), conv_sample)
```

```python
import functools
import math

import jax
import jax.numpy as jnp
import numpy as np
from jax import lax
from jax.experimental import pallas as pl
from jax.experimental.pallas import tpu as pltpu

F32 = jnp.float32
BF16 = jnp.bfloat16
I32 = jnp.int32

N_HEADS = 8
N_KV_HEADS = 2
N_IDX_HEADS = 8
IDX_DIM = 64
TOPK_MAX = 256
CONV_WIDTH = 31
N_SUBKEYS = 128
PEER_HEADS = 8
PEER_TOPK = 16
N_ADA = 6
EPS = 1e-6
MASKED_DIST = 1e33
INT_MIN = -(2 ** 31)
VMEM_LIMIT = 56 * 1024 * 1024
LANES = 128
SUBLANES = 8
BF16_TILE = 16


def _cparams(sem):
    return pltpu.CompilerParams(dimension_semantics=sem, vmem_limit_bytes=VMEM_LIMIT)


def _nt_dot(a, b):
    return lax.dot_general(a, b, (((1,), (1,)), ((), ())), preferred_element_type=F32)


def _mod_spec(mod, tm, d, tiles_per_group, chunk):
    if mod.ndim == 2:
        return pl.BlockSpec((tm, d), lambda i, *_: (i, chunk))
    return pl.BlockSpec((None, mod.shape[1], d), lambda i, *_: (i // tiles_per_group, 0, chunk))


COPY_STREAMS = 4


def _col_specs(tm, n, ns):
    return [pl.BlockSpec((tm, n // ns), lambda i, *_, k=k: (i, k)) for k in range(ns)]


def _piece_specs(tm, n, ns):
    return [pl.BlockSpec((tm, n // ns), lambda i, *_: (i, 0))] * ns


def _cat(refs):
    return refs[0][...] if len(refs) == 1 else jnp.concatenate([r[...] for r in refs], axis=1)


def _put(refs, val):
    w = val.shape[1] // len(refs)
    for k, r in enumerate(refs):
        r[...] = val[:, k * w:(k + 1) * w]


def _sortable(x):
    b = lax.bitcast_convert_type(x, I32)
    return jnp.where(b < 0, b ^ jnp.int32(0x7FFFFFFF), b)


def _ada_kernel(c_ref, w_ref, b_ref, o_ref):
    c = c_ref[...]
    s = c * jax.nn.sigmoid(c)
    o_ref[...] = jnp.dot(s.astype(BF16), w_ref[...].astype(BF16), preferred_element_type=F32) + b_ref[...]


def _ada(c, w_ada, b_ada):
    n, d = c.shape
    nd = w_ada.shape[1]
    return pl.pallas_call(
        _ada_kernel,
        grid=(nd // d,),
        in_specs=[pl.BlockSpec((n, d), lambda j: (0, 0)),
                  pl.BlockSpec((d, d), lambda j: (0, j)),
                  pl.BlockSpec((1, d), lambda j: (0, j))],
        out_specs=pl.BlockSpec((n, d), lambda j: (0, j)),
        out_shape=jax.ShapeDtypeStruct((n, nd), F32),
        compiler_params=_cparams(("arbitrary",)),
        name="ada",
    )(c, w_ada, b_ada.reshape(1, nd))


def _inproj_kernel(*refs, d, dh, dkv, dqi, ns):
    x_refs = refs[:ns]
    shift_ref, scale_ref, g1_ref, gq_ref, gk_ref, w_ref = refs[ns:ns + 6]
    q_ref, k_ref, kb_ref, v_ref, vb_ref, qi_ref, kiwi_ref, kib_ref = refs[ns + 6:ns + 14]
    z_refs = refs[ns + 14:2 * ns + 14]
    ga_refs = refs[2 * ns + 14:3 * ns + 14]
    gb_refs = refs[3 * ns + 14:4 * ns + 14]
    x = _cat(x_refs)
    xn = x * lax.rsqrt(jnp.mean(x * x, axis=-1, keepdims=True) + EPS) * g1_ref[...]
    hb = (xn * (1.0 + scale_ref[...]) + shift_ref[...]).astype(BF16)

    def mm(lo, n):
        return jnp.dot(hb, w_ref[:, lo:lo + n], preferred_element_type=F32)

    def head_norm(t, g):
        return t * lax.rsqrt(jnp.mean(t * t, axis=-1, keepdims=True) + EPS) * g

    off = 0
    qf = mm(off, d)
    off += d
    for h in range(d // dh):
        qh = head_norm(qf[:, h * dh:(h + 1) * dh], gq_ref[...]) * (dh ** -0.5)
        q_ref[:, h * dh:(h + 1) * dh] = qh.astype(BF16)
    kf = mm(off, dkv)
    off += dkv
    for h in range(dkv // dh):
        kh = head_norm(kf[:, h * dh:(h + 1) * dh], gk_ref[...])
        k_ref[:, h * dh:(h + 1) * dh] = kh
        kb_ref[:, h * dh:(h + 1) * dh] = kh.astype(BF16)
    vf = mm(off, dkv)
    off += dkv
    v_ref[...] = vf
    vb_ref[...] = vf.astype(BF16)
    qi_ref[...] = (mm(off, dqi) * (IDX_DIM ** -0.5)).astype(BF16)
    off += dqi
    kiwi = mm(off, LANES)
    off += LANES
    lane = lax.broadcasted_iota(I32, kiwi.shape, 1)
    is_wi = (lane >= IDX_DIM) & (lane < IDX_DIM + N_IDX_HEADS)
    kiwi_ref[...] = jnp.where(is_wi, kiwi * (N_IDX_HEADS ** -0.5), kiwi)
    kib_ref[...] = kiwi.astype(BF16)
    a = mm(off, d)
    off += d
    g = mm(off, d)
    off += d
    _put(z_refs, a * jax.nn.sigmoid(g))
    _put(ga_refs, jax.nn.sigmoid(mm(off, d)))
    off += d
    _put(gb_refs, jax.nn.sigmoid(mm(off, d)))


def _inproj(x, mod3, g1, gq, gk, w1, *, tm, tiles_per_group):
    m, d = x.shape
    dh = gq.shape[-1]
    dkv = N_KV_HEADS * dh
    dqi = N_IDX_HEADS * IDX_DIM
    nw = w1.shape[1]
    row = lambda n: pl.BlockSpec((tm, n), lambda i: (i, 0))
    const = lambda s: pl.BlockSpec(s, lambda i: (0,) * len(s))
    mod = lambda c: _mod_spec(mod3, tm, d, tiles_per_group, c)
    shp = lambda n, t: jax.ShapeDtypeStruct((m, n), t)
    ns = COPY_STREAMS
    outs = pl.pallas_call(
        functools.partial(_inproj_kernel, d=d, dh=dh, dkv=dkv, dqi=dqi, ns=ns),
        grid=(m // tm,),
        in_specs=_col_specs(tm, d, ns)
        + [mod(0), mod(1), const((1, d)), const((1, dh)), const((1, dh)), const((d, nw))],
        out_specs=[row(d), row(dkv), row(dkv), row(dkv), row(dkv), row(dqi), row(LANES), row(LANES)]
        + _piece_specs(tm, d, ns) * 3,
        out_shape=[shp(d, BF16), shp(dkv, F32), shp(dkv, BF16), shp(dkv, F32), shp(dkv, BF16), shp(dqi, BF16),
                   shp(LANES, F32), shp(LANES, BF16)] + [shp(d // ns, F32)] * (3 * ns),
        compiler_params=_cparams(("parallel",)),
        name="inproj",
    )(*([x] * ns), mod3, mod3, g1, gq, gk, w1)
    return tuple(outs[:8]) + (outs[8:8 + ns], outs[8 + ns:8 + 2 * ns], outs[8 + 2 * ns:])


def _count_rows(pred):
    rows, tq = pred.shape
    return jnp.sum(jnp.where(pred, 1, 0).astype(I32).reshape(rows // SUBLANES, SUBLANES, tq), axis=0)


def _topk_threshold(key_ref, n_chunks, kc, topk, n_index_bits, thr_ref, j_ref):
    tq = key_ref.shape[1]

    def count(pred_fn):
        def body(c, acc):
            blk = key_ref[pl.ds(pl.multiple_of(c * kc, kc), kc), :]
            row0 = c * kc
            return acc + _count_rows(pred_fn(blk, row0))
        acc = lax.fori_loop(0, n_chunks, body, jnp.zeros((SUBLANES, tq), I32))
        return jnp.sum(acc, axis=0, keepdims=True)

    def value_step(it, thr_u):
        bit = jnp.left_shift(jnp.int32(1), 31 - it)
        cand_s = (thr_u | bit) ^ jnp.int32(INT_MIN)
        cnt = count(lambda blk, row0: blk >= cand_s)
        return jnp.where(cnt >= topk, thr_u | bit, thr_u)

    thr_u = lax.fori_loop(0, 32, value_step, jnp.zeros((1, tq), I32))
    thr = thr_u ^ jnp.int32(INT_MIN)
    thr_ref[...] = thr
    n_gt = count(lambda blk, row0: blk > thr)
    n_ge = count(lambda blk, row0: blk >= jnp.maximum(thr, jnp.int32(INT_MIN + 1)))
    need = topk - n_gt
    j_ref[...] = jnp.where(thr == jnp.int32(INT_MIN), -1, jnp.int32(2 ** n_index_bits))

    @pl.when(jnp.max(n_ge) > topk)
    def _():
        def index_step(it, j):
            cand = j | jnp.left_shift(jnp.int32(1), n_index_bits - 1 - it)

            def pred(blk, row0):
                sidx = row0 + lax.broadcasted_iota(I32, blk.shape, 0)
                return (blk == thr) & (sidx < cand)
            return jnp.where(count(pred) < need, cand, j)
        j = lax.fori_loop(0, n_index_bits, index_step, jnp.zeros((1, tq), I32))
        j_ref[...] = jnp.where(thr == jnp.int32(INT_MIN), -1, j)


def _pattn_kernel(qi_ref, ki_ref, wit_ref, q_ref, k_ref, vt_ref, o_ref,
                  key_ref, dist_ref, lg_ref, acc_ref, thr_ref, j_ref, *, tq, dh, topk, n_index_bits, slopes):
    i = pl.program_id(1)
    n_chunks = i + 1
    kc = tq
    t0 = i * tq
    col = lax.broadcasted_iota(I32, (kc, tq), 1)
    row = lax.broadcasted_iota(I32, (kc, tq), 0)

    qis = [qi_ref[:, h * IDX_DIM:(h + 1) * IDX_DIM] for h in range(N_IDX_HEADS)]

    def score_chunk(c, carry):
        r0 = pl.multiple_of(c * kc, kc)
        kic = ki_ref[pl.ds(r0, kc), 0:IDX_DIM]
        acc = jnp.zeros((kc, tq), F32)
        for h in range(N_IDX_HEADS):
            s = _nt_dot(kic, qis[h])
            acc = acc + jnp.maximum(s, 0.0) * wit_ref[h:h + 1, :]
        causal = (row + c * kc) <= (col + t0)
        key_ref[pl.ds(r0, kc), :] = jnp.where(causal, _sortable(acc), jnp.int32(INT_MIN))
        return carry

    lax.fori_loop(0, n_chunks, score_chunk, 0)
    _topk_threshold(key_ref, n_chunks, kc, topk, n_index_bits, thr_ref, j_ref)
    thr = thr_ref[...]
    jj = j_ref[...]

    def dist_chunk(c, carry):
        r0 = pl.multiple_of(c * kc, kc)
        key = key_ref[pl.ds(r0, kc), :]
        sidx = row + c * kc
        sel = (key > thr) | ((key == thr) & (sidx <= jj))
        dist = ((col + t0) - sidx).astype(F32)
        dist_ref[pl.ds(r0, kc), :] = jnp.where(sel, dist, MASKED_DIST)
        return carry

    lax.fori_loop(0, n_chunks, dist_chunk, 0)

    group = N_HEADS // N_KV_HEADS
    heads = range(N_HEADS)
    qhs = [q_ref[:, h * dh:(h + 1) * dh] for h in heads]

    def fold8(x, op):
        return op(x.reshape(kc // SUBLANES, SUBLANES, tq), axis=0)

    def logit_chunk(c, ms):
        r0 = pl.multiple_of(c * kc, kc)
        dist = dist_ref[pl.ds(r0, kc), :]
        out = []
        for h in heads:
            v = h // group
            s = _nt_dot(k_ref[pl.ds(r0, kc), v * dh:(v + 1) * dh], qhs[h])
            lg = s - slopes[h] * dist
            lg_ref[h, pl.ds(r0, kc), :] = lg
            out.append(jnp.maximum(ms[h], fold8(lg, jnp.max)))
        return tuple(out)

    m8 = lax.fori_loop(0, n_chunks, logit_chunk,
                       tuple(jnp.full((SUBLANES, tq), -jnp.inf, F32) for _ in heads))
    mx = [jnp.max(m, axis=0, keepdims=True) for m in m8]
    acc_ref[...] = jnp.zeros(acc_ref.shape, F32)

    def pv_chunk(c, ls):
        r0 = pl.multiple_of(c * kc, kc)
        out = []
        for h in heads:
            v = h // group
            e = jnp.exp(lg_ref[h, pl.ds(r0, kc), :] - mx[h])
            out.append(ls[h] + fold8(e, jnp.sum))
            acc_ref[h] += jnp.dot(vt_ref[c, v * dh:(v + 1) * dh, :], e.astype(BF16),
                                  preferred_element_type=F32)
        return tuple(out)

    l8 = lax.fori_loop(0, n_chunks, pv_chunk, tuple(jnp.zeros((SUBLANES, tq), F32) for _ in heads))
    for h in heads:
        inv = 1.0 / jnp.sum(l8[h], axis=0, keepdims=True)
        o_ref[:, h * dh:(h + 1) * dh] = (acc_ref[h] * inv).T.astype(BF16)


def _pattn(qi, kib, wit, q, kb, vt, *, batch, seq, tq, topk):
    m, d = q.shape
    dh = d // N_HEADS
    dkv = N_KV_HEADS * dh
    dqi = qi.shape[1]
    nq = seq // tq
    slopes = tuple(float(2.0 ** (-8.0 * (h + 1) / N_HEADS)) for h in range(N_HEADS))
    n_index_bits = max(1, int(math.ceil(math.log2(seq))))
    qrow = lambda n: pl.BlockSpec((tq, n), lambda b, i: (b * nq + i, 0))
    return pl.pallas_call(
        functools.partial(_pattn_kernel, tq=tq, dh=dh, topk=topk, n_index_bits=n_index_bits, slopes=slopes),
        grid=(batch, nq),
        in_specs=[qrow(dqi),
                  pl.BlockSpec((seq, LANES), lambda b, i: (b, 0)),
                  pl.BlockSpec((N_IDX_HEADS, tq), lambda b, i: (0, b * nq + i)),
                  qrow(d),
                  pl.BlockSpec((seq, dkv), lambda b, i: (b, 0)),
                  pl.BlockSpec((None, nq, dkv, tq), lambda b, i: (b, 0, 0, 0))],
        out_specs=qrow(d),
        out_shape=jax.ShapeDtypeStruct((m, d), BF16),
        scratch_shapes=[pltpu.VMEM((seq, tq), I32), pltpu.VMEM((seq, tq), F32),
                        pltpu.VMEM((N_HEADS, seq, tq), F32), pltpu.VMEM((N_HEADS, dh, tq), F32),
                        pltpu.VMEM((1, tq), I32), pltpu.VMEM((1, tq), I32)],
        compiler_params=_cparams(("parallel", "arbitrary")),
        name="pattn",
    )(qi, kib, wit, q, kb, vt)


def _sidx_kernel(pt_ref, qi_ref, w_ref, kin_ref, *rest, n_pages, page, t_new, lp):
    pages = rest[:n_pages]
    key_ref = rest[n_pages]
    kall_ref = rest[n_pages + 1]
    past = n_pages * page
    for p in range(n_pages):
        kall_ref[:, p * page:(p + 1) * page] = pages[p][...].astype(BF16)
    kall_ref[:, past:lp] = jnp.zeros((IDX_DIM, lp - past), BF16)
    kall_ref[:, past:past + t_new] = kin_ref[...]
    s = jnp.dot(qi_ref[...], kall_ref[...], preferred_element_type=F32)
    sw = jnp.maximum(s, 0.0) * w_ref[...]
    isc = jnp.sum(sw.reshape(t_new, N_IDX_HEADS, lp), axis=1)
    sidx = lax.broadcasted_iota(I32, (t_new, lp), 1)
    t = lax.broadcasted_iota(I32, (t_new, lp), 0)
    causal = sidx <= past + t
    key_ref[...] = jnp.where(causal, _sortable(isc), jnp.int32(INT_MIN))


def _sidx(page_table, qi32, w32, kib_new_t, cache_idx_k_t, *, lp):
    bd, n_pages = page_table.shape
    page = cache_idx_k_t.shape[2]
    t_new = kib_new_t.shape[2]
    rows = qi32.shape[1]
    page_spec = lambda p: pl.BlockSpec((None, IDX_DIM, page), lambda b, pt, p=p: (pt[b, p], 0, 0))
    grid_spec = pltpu.PrefetchScalarGridSpec(
        num_scalar_prefetch=1,
        grid=(bd,),
        in_specs=[pl.BlockSpec((None, rows, IDX_DIM), lambda b, pt: (b, 0, 0)),
                  pl.BlockSpec((None, rows, 1), lambda b, pt: (b, 0, 0)),
                  pl.BlockSpec((None, IDX_DIM, t_new), lambda b, pt: (b, 0, 0))]
        + [page_spec(p) for p in range(n_pages)],
        out_specs=pl.BlockSpec((None, t_new, lp), lambda b, pt: (b, 0, 0)),
        scratch_shapes=[pltpu.VMEM((IDX_DIM, lp), BF16)],
    )
    return pl.pallas_call(
        functools.partial(_sidx_kernel, n_pages=n_pages, page=page, t_new=t_new, lp=lp),
        grid_spec=grid_spec,
        out_shape=jax.ShapeDtypeStruct((bd, t_new, lp), I32),
        compiler_params=_cparams(("arbitrary",)),
        name="sidx",
    )(page_table, qi32, w32, kib_new_t, *([cache_idx_k_t] * n_pages))


def _sthr_kernel(key_ref, thr_ref, j_ref, *, kc, topk, n_index_bits):
    _topk_threshold(key_ref, key_ref.shape[0] // kc, kc, topk, n_index_bits, thr_ref, j_ref)


def _sthr(keys_t, *, tq, kc, topk):
    lp, nq = keys_t.shape
    n_index_bits = max(1, int(math.ceil(math.log2(lp))))
    return pl.pallas_call(
        functools.partial(_sthr_kernel, kc=kc, topk=topk, n_index_bits=n_index_bits),
        grid=(nq // tq,),
        in_specs=[pl.BlockSpec((lp, tq), lambda i: (0, i))],
        out_specs=[pl.BlockSpec((1, tq), lambda i: (0, i))] * 2,
        out_shape=[jax.ShapeDtypeStruct((1, nq), I32)] * 2,
        compiler_params=_cparams(("parallel",)),
        name="sthr",
    )(keys_t)


def _sattn_kernel(pt_ref, q_ref, key_ref, thr_ref, j_ref, tpos_ref, slope_ref, kn_ref, vn_ref, *rest,
                  n_pages, page, t_new, lp, dh):
    kpages = rest[:n_pages]
    vpages = rest[n_pages:2 * n_pages]
    o_ref = rest[2 * n_pages]
    kall_ref, vall_ref = rest[2 * n_pages + 1:]
    past = n_pages * page
    for v in range(N_KV_HEADS):
        for p in range(n_pages):
            kall_ref[v, p * page:(p + 1) * page, :] = kpages[p][pl.ds(v, page, stride=N_KV_HEADS), :].astype(BF16)
            vall_ref[v, p * page:(p + 1) * page, :] = vpages[p][pl.ds(v, page, stride=N_KV_HEADS), :].astype(BF16)
        kall_ref[v, past:lp, :] = jnp.zeros((lp - past, dh), BF16)
        vall_ref[v, past:lp, :] = jnp.zeros((lp - past, dh), BF16)
        kall_ref[v, past:past + t_new, :] = kn_ref[:, v * dh:(v + 1) * dh]
        vall_ref[v, past:past + t_new, :] = vn_ref[:, v * dh:(v + 1) * dh]

    key = key_ref[...]
    thr = thr_ref[...]
    sidx = lax.broadcasted_iota(I32, key.shape, 1)
    sel = (key > thr) | ((key == thr) & (sidx <= j_ref[...]))
    dist = jnp.where(sel, (tpos_ref[...] - sidx).astype(F32), MASKED_DIST)
    for v in range(N_KV_HEADS):
        s = _nt_dot(q_ref[v], kall_ref[v])
        lg = s - slope_ref[v] * dist
        m = jnp.max(lg, axis=-1, keepdims=True)
        e = jnp.exp(lg - m)
        l = jnp.sum(e, axis=-1, keepdims=True)
        o = jnp.dot(e.astype(BF16), vall_ref[v], preferred_element_type=F32)
        o_ref[v] = (o / l).astype(BF16)


def _sattn(page_table, q16, keys16, thr16, j16, tpos16, slope16, kb_new, vb_new, cache_k, cache_v, *, lp):
    bd, n_pages = page_table.shape
    dh = cache_k.shape[2]
    page = cache_k.shape[1] // N_KV_HEADS
    dkv = N_KV_HEADS * dh
    t_new = kb_new.shape[1]
    rows = q16.shape[2]
    per_seq = lambda *s: pl.BlockSpec((None,) + s, lambda b, pt: (b,) + (0,) * len(s))
    page_spec = lambda p: pl.BlockSpec((None, page * N_KV_HEADS, dh), lambda b, pt, p=p: (pt[b, p], 0, 0))
    grid_spec = pltpu.PrefetchScalarGridSpec(
        num_scalar_prefetch=1,
        grid=(bd,),
        in_specs=[per_seq(N_KV_HEADS, rows, dh), per_seq(rows, lp), per_seq(rows, 1), per_seq(rows, 1),
                  pl.BlockSpec((rows, 1), lambda b, pt: (0, 0)),
                  pl.BlockSpec((N_KV_HEADS, rows, 1), lambda b, pt: (0, 0, 0)),
                  per_seq(t_new, dkv), per_seq(t_new, dkv)]
        + [page_spec(p) for p in range(n_pages)] * 2,
        out_specs=per_seq(N_KV_HEADS, rows, dh),
        scratch_shapes=[pltpu.VMEM((N_KV_HEADS, lp, dh), BF16), pltpu.VMEM((N_KV_HEADS, lp, dh), BF16)],
    )
    return pl.pallas_call(
        functools.partial(_sattn_kernel, n_pages=n_pages, page=page, t_new=t_new, lp=lp, dh=dh),
        grid_spec=grid_spec,
        out_shape=jax.ShapeDtypeStruct((bd, N_KV_HEADS, rows, dh), BF16),
        compiler_params=_cparams(("arbitrary",)),
        name="sattn",
    )(page_table, q16, keys16, thr16, j16, tpos16, slope16, kb_new, vb_new,
      *([cache_k] * n_pages), *([cache_v] * n_pages))


def _ln_swish(y, g, b):
    yc = y - jnp.mean(y, axis=-1, keepdims=True)
    yn = yc * lax.rsqrt(jnp.mean(yc * yc, axis=-1, keepdims=True) + EPS) * g + b
    return yn * jax.nn.sigmoid(yn)


def _pconv_kernel(*refs, tm, halo, ns):
    z_refs = refs[:ns]
    halo_refs = refs[ns:2 * ns]
    w_ref, b_ref, g_ref, beta_ref, o_ref, zs_ref, sh_ref, y_ref = refs[2 * ns:]
    d = w_ref.shape[1]
    n = halo + tm
    first = pl.program_id(1) == 0
    zs_ref[0:halo, :] = jnp.where(first, 0.0, _cat(halo_refs))
    zs_ref[halo:n, :] = _cat(z_refs)
    zs_ref[n:n + SUBLANES, :] = jnp.zeros((SUBLANES, d), F32)
    for b in range(SUBLANES):
        sh_ref[b] = zs_ref[b:b + n, :]
    lead = halo - (CONV_WIDTH - 1)
    rows = 64
    for lc in range(d // LANES):
        ls = slice(lc * LANES, (lc + 1) * LANES)

        def row_chunk(rc, carry, ls=ls):
            r0 = pl.multiple_of(rc * rows, rows)
            n_acc = 4
            accs = [jnp.zeros((rows, LANES), F32) + b_ref[:, ls]] + [jnp.zeros((rows, LANES), F32)] * (n_acc - 1)
            for j in range(CONV_WIDTH):
                off = lead + j
                a = off - off % SUBLANES
                accs[j % n_acc] = accs[j % n_acc] + (w_ref[j:j + 1, ls]
                                                     * sh_ref[off % SUBLANES, pl.ds(r0 + a, rows), ls])
            y_ref[pl.ds(r0, rows), ls] = (accs[0] + accs[1]) + (accs[2] + accs[3])
            return carry

        lax.fori_loop(0, tm // rows, row_chunk, 0)
    o_ref[...] = _ln_swish(y_ref[...], g_ref[...], beta_ref[...]).astype(BF16)


def _pconv(zs, w_dw, b_dw, ln_g, ln_b, *, batch, seq, tm):
    ns = len(zs)
    m, dp = zs[0].shape
    d = dp * ns
    halo = 32
    nt = seq // tm
    hb = tm // halo
    const = lambda s: pl.BlockSpec(s, lambda b, i: (0,) * len(s))
    return pl.pallas_call(
        functools.partial(_pconv_kernel, tm=tm, halo=halo, ns=ns),
        grid=(batch, nt),
        in_specs=[pl.BlockSpec((tm, dp), lambda b, i: (b * nt + i, 0))] * ns
        + [pl.BlockSpec((halo, dp), lambda b, i: (jnp.maximum((b * nt + i) * hb - 1, 0), 0))] * ns
        + [const((CONV_WIDTH, d)), const((1, d)), const((1, d)), const((1, d))],
        out_specs=pl.BlockSpec((tm, d), lambda b, i: (b * nt + i, 0)),
        out_shape=jax.ShapeDtypeStruct((m, d), BF16),
        scratch_shapes=[pltpu.VMEM((halo + tm + SUBLANES, d), F32), pltpu.VMEM((SUBLANES, halo + tm, d), F32),
                        pltpu.VMEM((tm, d), F32)],
        compiler_params=_cparams(("parallel", "arbitrary")),
        name="pconv",
    )(*zs, *zs, w_dw, b_dw.reshape(1, d), ln_g.reshape(1, d), ln_b.reshape(1, d))


def _sconv_kernel(z_ref, st_ref, w_ref, b_ref, g_ref, beta_ref, o_ref, so_ref, *, t_new):
    left = CONV_WIDTH - 1

    def zp(r):
        return st_ref[:, r, :] if r < left else z_ref[:, r - left, :]

    for t in range(t_new):
        acc = jnp.zeros(zp(0).shape, F32) + b_ref[...]
        for j in range(CONV_WIDTH):
            acc = acc + w_ref[j:j + 1, :] * zp(t + j)
        o_ref[:, t, :] = _ln_swish(acc, g_ref[...], beta_ref[...])
    for r in range(left):
        so_ref[:, r, :] = zp(r + t_new)


def _sconv(z3, state, w_dw, b_dw, ln_g, ln_b, *, bs):
    bd, t_new, d = z3.shape
    left = state.shape[1]
    const = lambda s: pl.BlockSpec(s, lambda i: (0,) * len(s))
    return pl.pallas_call(
        functools.partial(_sconv_kernel, t_new=t_new),
        grid=(bd // bs,),
        in_specs=[pl.BlockSpec((bs, t_new, d), lambda i: (i, 0, 0)),
                  pl.BlockSpec((bs, left, d), lambda i: (i, 0, 0)),
                  const((CONV_WIDTH, d)), const((1, d)), const((1, d)), const((1, d))],
        out_specs=[pl.BlockSpec((bs, t_new, d), lambda i: (i, 0, 0)),
                   pl.BlockSpec((bs, left, d), lambda i: (i, 0, 0))],
        out_shape=[jax.ShapeDtypeStruct((bd, t_new, d), F32), jax.ShapeDtypeStruct((bd, left, d), F32)],
        compiler_params=_cparams(("parallel",)),
        name="sconv",
    )(z3, state, w_dw, b_dw.reshape(1, d), ln_g.reshape(1, d), ln_b.reshape(1, d))


def _mid_kernel(*refs, ns):
    x_refs = refs[:ns]
    attn_ref, cv_ref = refs[ns:ns + 2]
    ga_refs = refs[ns + 2:2 * ns + 2]
    gb_refs = refs[2 * ns + 2:3 * ns + 2]
    gate1_ref, shift2_ref, scale2_ref, g2_ref, wo_ref, wc_ref, wout_ref, wq_ref = refs[3 * ns + 2:3 * ns + 10]
    x1_refs = refs[3 * ns + 10:4 * ns + 10]
    h2_ref = refs[4 * ns + 10]
    qp_refs = refs[4 * ns + 11:]
    ao = jnp.dot(attn_ref[...], wo_ref[...], preferred_element_type=F32)
    co = jnp.dot(cv_ref[...].astype(BF16), wc_ref[...], preferred_element_type=F32)
    merged = _cat(ga_refs) * ao + _cat(gb_refs) * co
    x1 = _cat(x_refs) + gate1_ref[...] * jnp.dot(merged.astype(BF16), wout_ref[...], preferred_element_type=F32)
    _put(x1_refs, x1)
    xn = x1 * lax.rsqrt(jnp.mean(x1 * x1, axis=-1, keepdims=True) + EPS) * g2_ref[...]
    h2 = (xn * (1.0 + scale2_ref[...]) + shift2_ref[...]).astype(BF16)
    h2_ref[...] = h2
    _put(qp_refs, jnp.dot(h2, wq_ref[...], preferred_element_type=F32).astype(BF16))


def _mid(x, attn, cv, gas, gbs, mod3, g2, wo, wc, wout, wq, *, tm, tiles_per_group):
    m, d = x.shape
    nq = wq.shape[1]
    ns = len(gas)
    row = lambda n: pl.BlockSpec((tm, n), lambda i: (i, 0))
    const = lambda s: pl.BlockSpec(s, lambda i: (0,) * len(s))
    mod = lambda c: _mod_spec(mod3, tm, d, tiles_per_group, c)
    outs = pl.pallas_call(
        functools.partial(_mid_kernel, ns=ns),
        grid=(m // tm,),
        in_specs=_col_specs(tm, d, ns) + [row(d), row(d)] + _piece_specs(tm, d, ns) * 2
        + [mod(2), mod(3), mod(4), const((1, d)), const((d, d)), const((d, d)), const((d, d)), const((d, nq))],
        out_specs=_piece_specs(tm, d, ns) + [row(d)] + _piece_specs(tm, nq, ns),
        out_shape=[jax.ShapeDtypeStruct((m, d // ns), F32)] * ns + [jax.ShapeDtypeStruct((m, d), BF16)]
        + [jax.ShapeDtypeStruct((m, nq // ns), BF16)] * ns,
        compiler_params=_cparams(("parallel",)),
        name="mid",
    )(*([x] * ns), attn, cv, *gas, *gbs, mod3, mod3, mod3, g2, wo, wc, wout, wq)
    return outs[:ns], outs[ns], outs[ns + 1:]


def _extract_top(xs, count, out_refs):
    def step(r, xs):
        nxt = []
        for x, out_ref in zip(xs, out_refs):
            rows, lanes = x.shape
            m = jnp.max(jnp.max(x.reshape(rows // SUBLANES, SUBLANES, lanes), axis=0), axis=0, keepdims=True)
            out_ref[pl.ds(r, 1), :] = m
            nxt.append(jnp.where(x == m, -jnp.inf, x))
        return tuple(nxt)

    lax.fori_loop(0, count, step, tuple(xs))


def _merge_exchange_pairs(n):
    t = n.bit_length() - 1
    pairs = []
    p = 1 << (t - 1)
    while p > 0:
        q, r, d = 1 << (t - 1), 0, p
        while d > 0:
            pairs += [(i, i + d) for i in range(n - d) if i & p == r]
            d, q, r = q - p, q >> 1, p
        p >>= 1
    return pairs


def _sorted_top_sublane_tiles(x, count):
    n = count
    v = [x[SUBLANES * j:SUBLANES * (j + 1), :] for j in range(n)]
    for i, j in _merge_exchange_pairs(n):
        v[i], v[j] = jnp.maximum(v[i], v[j]), jnp.minimum(v[i], v[j])
    shift = SUBLANES // 2
    while shift >= 1:
        v = [jnp.maximum(v[j], pltpu.roll(v[n - 1 - j], shift, 0)) for j in range(n)]
        k = n // 2
        while k >= 1:
            for i in range(n):
                if i & k == 0:
                    v[i], v[i + k] = jnp.maximum(v[i], v[i + k]), jnp.minimum(v[i], v[i + k])
            k //= 2
        shift //= 2
    return v


def _gelu_tanh(x):
    return (0.5 * x) * (1.0 + jnp.tanh(x * (0.7978845608028654 + 0.035677408136300125 * (x * x))))


def _peer_kernel(h2_ref, sk1_ref, sk2_ref, gate2_ref, *refs, tm, te, half, pairs, parts, streams, ns):
    qp_refs = refs[:ns]
    x1_refs = refs[ns:2 * ns]
    refs = refs[2 * ns:]
    u_refs = refs[:streams]
    vt_refs = refs[streams:2 * streams]
    (y_ref, n1_ref, e1_ref, r2_ref, e2_ref, acc_ref, work_ref, cand_ref, topc_ref) = refs[2 * streams:2 * streams + 9]
    part_refs = refs[2 * streams + 9:]
    e_step = pl.program_id(1)
    nk = N_SUBKEYS
    at_refs = part_refs[:parts]
    ht_refs = part_refs[parts:]
    ts = te // streams

    @pl.when(e_step == 0)
    def _():
        acc_ref[...] = jnp.zeros(acc_ref.shape, F32)
        qp = _cat(qp_refs)
        for h in range(PEER_HEADS):
            base = h * 2 * half
            q1 = qp[:, base:base + half].astype(BF16)
            q2 = qp[:, base + half:base + 2 * half].astype(BF16)
            work_ref[0] = _nt_dot(sk1_ref[...], q1)
            work_ref[1] = _nt_dot(sk2_ref[...], q2)
            for lc in range(tm // LANES):
                ls = slice(lc * LANES, (lc + 1) * LANES)
                s1 = work_ref[0, :, ls]
                s2 = work_ref[1, :, ls]
                top1 = _sorted_top_sublane_tiles(s1, PEER_TOPK)
                top2 = _sorted_top_sublane_tiles(s2, PEER_TOPK)
                cand_ref[...] = jnp.full(cand_ref.shape, -jnp.inf, F32)
                for n, (a, b) in enumerate(pairs):
                    cand_ref[n:n + 1, :] = top1[a][0:1, :] + top2[b][0:1, :]
                _extract_top((cand_ref[...],), PEER_TOPK, (topc_ref,))
                topc = topc_ref[...]
                best = topc[0:1, :]
                norm = jnp.sum(jnp.exp(topc - best), axis=0, keepdims=True)
                thr = jnp.broadcast_to(topc[PEER_TOPK - 1:PEER_TOPK, :], (SUBLANES, LANES))
                tiles = (nk // SUBLANES, SUBLANES, LANES)
                s1t = s1.reshape(tiles)
                s2t = s2.reshape(tiles)
                n1 = jnp.zeros(tiles, F32)
                r2 = jnp.zeros(tiles, F32)
                for b in range(PEER_TOPK):
                    n1 = jnp.where((s1t + top2[b]) >= thr, float(b + 1), n1)
                    r2 = jnp.where(top2[b] > s2t, float(b + 1), r2)
                n1_ref[h, :, ls] = n1.reshape(nk, LANES)
                r2_ref[h, :, ls] = r2.reshape(nk, LANES).astype(BF16)
                e1 = jnp.where(s1t >= top1[PEER_TOPK - 1], jnp.exp(s1t - top1[0]), 0.0)
                e1_ref[h, :, ls] = e1.reshape(nk, LANES)
                e2 = jnp.exp(s2t - top2[0]) / jnp.broadcast_to(norm, (SUBLANES, LANES))
                e2_ref[h, :, ls] = e2.reshape(nk, LANES).astype(BF16)

    rows = nk // 2
    n_i1 = te // nk
    tp = te // parts

    def pre_act(p):
        for k in range(tp // ts):
            at_refs[p][k * ts:(k + 1) * ts, :] = _nt_dot(u_refs[p * (tp // ts) + k][...], h2_ref[...])

    def gate_part(p):
        for jl in range(tp // nk):
            i1 = e_step * n_i1 + p * (tp // nk) + jl
            n1_rows = [n1_ref[h, pl.ds(i1, 1), :] for h in range(PEER_HEADS)]
            w1_rows = [e1_ref[h, pl.ds(i1, 1), :] for h in range(PEER_HEADS)]
            for lc in range(tm // LANES):
                ls = slice(lc * LANES, (lc + 1) * LANES)
                bc = lambda r: jnp.broadcast_to(r[:, ls], (BF16_TILE, LANES)).astype(BF16)[None]
                n1 = [bc(r) for r in n1_rows]
                w1 = [bc(r) for r in w1_rows]
                for sc in range(nk // rows):
                    rs = slice(sc * rows, (sc + 1) * rows)
                    tiles = (rows // BF16_TILE, BF16_TILE, LANES)
                    g = jnp.zeros(tiles, BF16)
                    for h in range(PEER_HEADS):
                        sel = r2_ref[h, rs, ls].reshape(tiles) < n1[h]
                        g = g + jnp.where(sel, e2_ref[h, rs, ls].reshape(tiles) * w1[h], jnp.zeros((), BF16))
                    r0 = jl * nk + sc * rows
                    act = _gelu_tanh(at_refs[p][r0:r0 + rows, ls]).astype(BF16)
                    ht_refs[p][r0:r0 + rows, ls] = g.reshape(rows, LANES) * act

    def accumulate(p):
        upd = None
        for k in range(tp // ts):
            part = jnp.dot(vt_refs[p * (tp // ts) + k][...], ht_refs[p][k * ts:(k + 1) * ts, :],
                           preferred_element_type=F32)
            upd = part if upd is None else upd + part
        acc_ref[...] += upd

    pre_act(0)
    for p in range(parts):
        if p + 1 < parts:
            pre_act(p + 1)
        gate_part(p)
        accumulate(p)

    @pl.when(e_step == pl.num_programs(1) - 1)
    def _():
        y_ref[...] = _cat(x1_refs) + gate2_ref[...] * acc_ref[...].T


def _peer(h2, qps, sk1, sk2, u, vt, x1s, mod3, *, tm, te, tiles_per_group):
    ns = len(x1s)
    m, d = h2.shape
    nq = qps[0].shape[1] * ns
    n_exp = u.shape[0]
    half = sk1.shape[1]
    pairs = tuple((a, b) for a in range(PEER_TOPK) for b in range(PEER_TOPK) if (a + 1) * (b + 1) <= PEER_TOPK)
    n_cand = -(-len(pairs) // SUBLANES) * SUBLANES
    row = lambda n: pl.BlockSpec((tm, n), lambda i, e: (i, 0))
    const = lambda s: pl.BlockSpec(s, lambda i, e: (0,) * len(s))
    parts = 2
    ts = vt.shape[2]
    streams = te // ts
    u_spec = lambda k: pl.BlockSpec((ts, d), lambda i, e, k=k: (e * streams + k, 0))
    vt_spec = lambda k: pl.BlockSpec((None, d, ts), lambda i, e, k=k: (e * streams + k, 0, 0))
    return pl.pallas_call(
        functools.partial(_peer_kernel, tm=tm, te=te, half=half, pairs=pairs, parts=parts, streams=streams, ns=ns),
        grid=(m // tm, n_exp // te),
        in_specs=[row(d), const(sk1.shape), const(sk2.shape), _mod_spec(mod3, tm, d, tiles_per_group, 5)]
        + _piece_specs(tm, nq, ns) + _piece_specs(tm, d, ns)
        + [u_spec(k) for k in range(streams)] + [vt_spec(k) for k in range(streams)],
        out_specs=row(d),
        out_shape=jax.ShapeDtypeStruct((m, d), F32),
        scratch_shapes=[pltpu.VMEM((PEER_HEADS, N_SUBKEYS, tm), F32)] * 2
        + [pltpu.VMEM((PEER_HEADS, N_SUBKEYS, tm), BF16)] * 2
        + [pltpu.VMEM((d, tm), F32),
           pltpu.VMEM((2, N_SUBKEYS, tm), F32),
           pltpu.VMEM((n_cand, LANES), F32), pltpu.VMEM((PEER_TOPK, LANES), F32)]
        + [pltpu.VMEM((te // parts, tm), F32)] * parts + [pltpu.VMEM((te // parts, tm), BF16)] * parts,
        compiler_params=_cparams(("parallel", "arbitrary")),
        name="peer",
    )(h2, sk1, sk2, mod3, *qps, *x1s, *([u] * streams), *([vt] * streams))


def _pack_w_in(w_in, d, dkv, dqi):
    o = 0
    cols = []
    for n in (d, dkv, dkv, dqi):
        cols.append(w_in[:, o:o + n])
        o += n
    kiwi = w_in[:, o:o + IDX_DIM + N_IDX_HEADS]
    o += IDX_DIM + N_IDX_HEADS
    cols.append(jnp.pad(kiwi, ((0, 0), (0, LANES - kiwi.shape[1]))))
    for n in (d, d, d, d):
        cols.append(w_in[:, o:o + n])
        o += n
    return jnp.concatenate(cols, axis=1).astype(BF16)


def kernel(x_prompt, x_sample, cache_k, cache_v, cache_idx_k, state_conv, page_table, c_prompt, c_sample, w_ada, b_ada, g_norm1, g_norm2, w_in, g_q, g_k, w_o_attn, w_dw, b_dw, ln_g, ln_b, w_conv_out, w_out, w_q_peer, sub_keys1, sub_keys2, u_emb, v_emb):
    b, s, d = x_prompt.shape
    bd, t_new, _ = x_sample.shape
    dh = g_q.shape[0]
    dkv = N_KV_HEADS * dh
    dqi = N_IDX_HEADS * IDX_DIM
    n_pages = page_table.shape[1]
    page = cache_k.shape[1]
    past = n_pages * page
    mp, ms = b * s, bd * t_new
    group = N_HEADS // N_KV_HEADS

    tm_p = min(256, s)
    tm_s = min(256, ms)
    tq = min(256, s)
    tm_peer = min(512, ms, mp)
    te = 1024

    w1 = _pack_w_in(w_in, d, dkv, dqi)
    wo = w_o_attn.astype(BF16)
    wc = w_conv_out.astype(BF16)
    wout = w_out.astype(BF16)
    wq = w_q_peer.astype(BF16)
    sk1 = sub_keys1.astype(BF16)
    sk2 = sub_keys2.astype(BF16)
    u = u_emb.astype(BF16)
    peer_slab = te // 4
    vt = jnp.swapaxes(v_emb.astype(BF16).reshape(-1, peer_slab, d), 1, 2)
    g1 = g_norm1.reshape(1, d)
    g2 = g_norm2.reshape(1, d)
    gq = g_q.reshape(1, dh)
    gk = g_k.reshape(1, dh)

    mod = _ada(jnp.concatenate([c_prompt, c_sample], axis=0), w_ada, b_ada)
    mod_p = mod[:b].reshape(b, 1, N_ADA * d)
    mod_s = jnp.repeat(mod[b:], t_new, axis=0)

    xp = x_prompt.reshape(mp, d)
    q, k, kb, v, vb, qi, kiwi, kib, z, ga, gb = _inproj(xp, mod_p, g1, gq, gk, w1, tm=tm_p,
                                                        tiles_per_group=s // tm_p)
    wit = kiwi[:, IDX_DIM:IDX_DIM + N_IDX_HEADS].T
    vbt = jnp.swapaxes(vb.reshape(b, s // tq, tq, dkv), 2, 3)
    attn = _pattn(qi, kib, wit, q, kb, vbt, batch=b, seq=s, tq=tq, topk=min(TOPK_MAX, s // 4))
    cv = _pconv(z, w_dw, b_dw, ln_g, ln_b, batch=b, seq=s, tm=tm_p)
    x1, h2, qp = _mid(xp, attn, cv, ga, gb, mod_p, g2, wo, wc, wout, wq, tm=tm_p, tiles_per_group=s // tm_p)
    y_prompt = _peer(h2, qp, sk1, sk2, u, vt, x1, mod_p, tm=tm_peer, te=te, tiles_per_group=s // tm_peer)
    k_prompt = k.reshape(b, s, N_KV_HEADS, dh)
    v_prompt = v.reshape(b, s, N_KV_HEADS, dh)
    idx_k_prompt = kiwi[:, :IDX_DIM].reshape(b, s, IDX_DIM)
    conv_prompt = jnp.concatenate([zc.reshape(b, s, -1)[:, s - (CONV_WIDTH - 1):] for zc in z], axis=-1)

    xs = x_sample.reshape(ms, d)
    q, k, kb, v, vb, qi, kiwi, kib, z, ga, gb = _inproj(xs, mod_s, g1, gq, gk, w1, tm=tm_s, tiles_per_group=1)
    lp = -(-(past + t_new) // LANES) * LANES
    qi32 = qi.reshape(bd, t_new * N_IDX_HEADS, IDX_DIM)
    w32 = kiwi[:, IDX_DIM:IDX_DIM + N_IDX_HEADS].reshape(bd, t_new * N_IDX_HEADS, 1)
    kib_new_t = jnp.swapaxes(kib[:, :IDX_DIM].reshape(bd, t_new, IDX_DIM), 1, 2)
    keys = _sidx(page_table, qi32, w32, kib_new_t, jnp.swapaxes(cache_idx_k, 1, 2), lp=lp)
    keys_t = keys.reshape(ms, lp).T
    thr, jj = _sthr(keys_t, tq=min(256, ms), kc=LANES, topk=min(TOPK_MAX, (past + t_new) // 4))
    rep = lambda a: jnp.repeat(a.reshape(bd, t_new, -1), group, axis=1)
    q16 = q.reshape(bd, t_new, N_KV_HEADS, group, dh).transpose(0, 2, 1, 3, 4).reshape(bd, N_KV_HEADS,
                                                                                      t_new * group, dh)
    tpos16 = jnp.repeat(past + jnp.arange(t_new, dtype=I32), group).reshape(t_new * group, 1)
    slopes = jnp.exp2(-8.0 * jnp.arange(1, N_HEADS + 1, dtype=F32) / N_HEADS).reshape(N_KV_HEADS, 1, group)
    slope16 = jnp.broadcast_to(slopes, (N_KV_HEADS, t_new, group)).reshape(N_KV_HEADS, t_new * group, 1)
    o16 = _sattn(page_table, q16, rep(keys), rep(thr.reshape(ms, 1)), rep(jj.reshape(ms, 1)), tpos16, slope16,
                 kb.reshape(bd, t_new, dkv), vb.reshape(bd, t_new, dkv),
                 cache_k.reshape(-1, page * N_KV_HEADS, dh), cache_v.reshape(-1, page * N_KV_HEADS, dh), lp=lp)
    attn = o16.reshape(bd, N_KV_HEADS, t_new, group, dh).transpose(0, 2, 1, 3, 4).reshape(ms, d)
    z_s = jnp.concatenate(z, axis=1).reshape(bd, t_new, d)
    cv3, conv_sample = _sconv(z_s, state_conv, w_dw, b_dw, ln_g, ln_b, bs=min(8, bd))
    x1, h2, qp = _mid(xs, attn, cv3.reshape(ms, d), ga, gb, mod_s, g2, wo, wc, wout, wq, tm=tm_s,
                      tiles_per_group=1)
    y_sample = _peer(h2, qp, sk1, sk2, u, vt, x1, mod_s, tm=min(tm_peer, ms), te=te, tiles_per_group=1)

    return (y_prompt.reshape(b, s, d), y_sample.reshape(bd, t_new, d), k_prompt, v_prompt, idx_k_prompt,
            conv_prompt, k.reshape(bd, t_new, N_KV_HEADS, dh), v.reshape(bd, t_new, N_KV_HEADS, dh),
            kiwi[:, :IDX_DIM].reshape(bd, t_new, IDX_DIM), conv_sample)
```

```python
import functools
import math

import jax
import jax.numpy as jnp
import numpy as np
from jax import lax
from jax.experimental import pallas as pl
from jax.experimental.pallas import tpu as pltpu

F32 = jnp.float32
BF16 = jnp.bfloat16
I32 = jnp.int32

N_HEADS = 8
N_KV_HEADS = 2
N_IDX_HEADS = 8
IDX_DIM = 64
TOPK_MAX = 256
CONV_WIDTH = 31
N_SUBKEYS = 128
PEER_HEADS = 8
PEER_TOPK = 16
N_ADA = 6
EPS = 1e-6
MASKED_DIST = 1e33
INT_MIN = -(2 ** 31)
VMEM_LIMIT = 56 * 1024 * 1024
LANES = 128
SUBLANES = 8
BF16_TILE = 16


def _cparams(sem):
    return pltpu.CompilerParams(dimension_semantics=sem, vmem_limit_bytes=VMEM_LIMIT)


def _nt_dot(a, b):
    return lax.dot_general(a, b, (((1,), (1,)), ((), ())), preferred_element_type=F32)


def _mod_spec(mod, tm, d, tiles_per_group, chunk):
    if mod.ndim == 2:
        return pl.BlockSpec((tm, d), lambda i, *_: (i, chunk))
    return pl.BlockSpec((None, mod.shape[1], d), lambda i, *_: (i // tiles_per_group, 0, chunk))


def _sortable(x):
    b = lax.bitcast_convert_type(x, I32)
    return jnp.where(b < 0, b ^ jnp.int32(0x7FFFFFFF), b)


def _ada_kernel(c_ref, w_ref, b_ref, o_ref):
    c = c_ref[...]
    s = c * jax.nn.sigmoid(c)
    o_ref[...] = jnp.dot(s.astype(BF16), w_ref[...].astype(BF16), preferred_element_type=F32) + b_ref[...]


def _ada(c, w_ada, b_ada):
    n, d = c.shape
    nd = w_ada.shape[1]
    return pl.pallas_call(
        _ada_kernel,
        grid=(nd // d,),
        in_specs=[pl.BlockSpec((n, d), lambda j: (0, 0)),
                  pl.BlockSpec((d, d), lambda j: (0, j)),
                  pl.BlockSpec((1, d), lambda j: (0, j))],
        out_specs=pl.BlockSpec((n, d), lambda j: (0, j)),
        out_shape=jax.ShapeDtypeStruct((n, nd), F32),
        compiler_params=_cparams(("arbitrary",)),
        name="ada",
    )(c, w_ada, b_ada.reshape(1, nd))


def _inproj_kernel(x_ref, shift_ref, scale_ref, g1_ref, gq_ref, gk_ref, w_ref,
                   q_ref, k_ref, kb_ref, v_ref, vb_ref, qi_ref, kiwi_ref, kib_ref, z_ref, ga_ref, gb_ref,
                   *, d, dh, dkv, dqi):
    x = x_ref[...]
    xn = x * lax.rsqrt(jnp.mean(x * x, axis=-1, keepdims=True) + EPS) * g1_ref[...]
    hb = (xn * (1.0 + scale_ref[...]) + shift_ref[...]).astype(BF16)

    def mm(lo, n):
        return jnp.dot(hb, w_ref[:, lo:lo + n], preferred_element_type=F32)

    def head_norm(t, g):
        return t * lax.rsqrt(jnp.mean(t * t, axis=-1, keepdims=True) + EPS) * g

    off = 0
    qf = mm(off, d)
    off += d
    for h in range(d // dh):
        qh = head_norm(qf[:, h * dh:(h + 1) * dh], gq_ref[...]) * (dh ** -0.5)
        q_ref[:, h * dh:(h + 1) * dh] = qh.astype(BF16)
    kf = mm(off, dkv)
    off += dkv
    for h in range(dkv // dh):
        kh = head_norm(kf[:, h * dh:(h + 1) * dh], gk_ref[...])
        k_ref[:, h * dh:(h + 1) * dh] = kh
        kb_ref[:, h * dh:(h + 1) * dh] = kh.astype(BF16)
    vf = mm(off, dkv)
    off += dkv
    v_ref[...] = vf
    vb_ref[...] = vf.astype(BF16)
    qi_ref[...] = (mm(off, dqi) * (IDX_DIM ** -0.5)).astype(BF16)
    off += dqi
    kiwi = mm(off, LANES)
    off += LANES
    lane = lax.broadcasted_iota(I32, kiwi.shape, 1)
    is_wi = (lane >= IDX_DIM) & (lane < IDX_DIM + N_IDX_HEADS)
    kiwi_ref[...] = jnp.where(is_wi, kiwi * (N_IDX_HEADS ** -0.5), kiwi)
    kib_ref[...] = kiwi.astype(BF16)
    a = mm(off, d)
    off += d
    g = mm(off, d)
    off += d
    z_ref[...] = a * jax.nn.sigmoid(g)
    ga_ref[...] = jax.nn.sigmoid(mm(off, d))
    off += d
    gb_ref[...] = jax.nn.sigmoid(mm(off, d))


def _inproj(x, mod3, g1, gq, gk, w1, *, tm, tiles_per_group):
    m, d = x.shape
    dh = gq.shape[-1]
    dkv = N_KV_HEADS * dh
    dqi = N_IDX_HEADS * IDX_DIM
    nw = w1.shape[1]
    row = lambda n: pl.BlockSpec((tm, n), lambda i: (i, 0))
    const = lambda s: pl.BlockSpec(s, lambda i: (0,) * len(s))
    mod = lambda c: _mod_spec(mod3, tm, d, tiles_per_group, c)
    shp = lambda n, t: jax.ShapeDtypeStruct((m, n), t)
    return pl.pallas_call(
        functools.partial(_inproj_kernel, d=d, dh=dh, dkv=dkv, dqi=dqi),
        grid=(m // tm,),
        in_specs=[row(d), mod(0), mod(1), const((1, d)), const((1, dh)), const((1, dh)), const((d, nw))],
        out_specs=[row(d), row(dkv), row(dkv), row(dkv), row(dkv), row(dqi), row(LANES), row(LANES),
                   row(d), row(d), row(d)],
        out_shape=[shp(d, BF16), shp(dkv, F32), shp(dkv, BF16), shp(dkv, F32), shp(dkv, BF16), shp(dqi, BF16),
                   shp(LANES, F32), shp(LANES, BF16), shp(d, F32), shp(d, F32), shp(d, F32)],
        compiler_params=_cparams(("parallel",)),
        name="inproj",
    )(x, mod3, mod3, g1, gq, gk, w1)


def _count_rows(pred):
    rows, tq = pred.shape
    return jnp.sum(jnp.where(pred, 1, 0).astype(I32).reshape(rows // SUBLANES, SUBLANES, tq), axis=0)


def _topk_threshold(key_ref, n_chunks, kc, topk, n_index_bits, thr_ref, j_ref):
    tq = key_ref.shape[1]

    def count(pred_fn):
        def body(c, acc):
            blk = key_ref[pl.ds(pl.multiple_of(c * kc, kc), kc), :]
            row0 = c * kc
            return acc + _count_rows(pred_fn(blk, row0))
        acc = lax.fori_loop(0, n_chunks, body, jnp.zeros((SUBLANES, tq), I32))
        return jnp.sum(acc, axis=0, keepdims=True)

    def value_step(it, thr_u):
        bit = jnp.left_shift(jnp.int32(1), 31 - it)
        cand_s = (thr_u | bit) ^ jnp.int32(INT_MIN)
        cnt = count(lambda blk, row0: blk >= cand_s)
        return jnp.where(cnt >= topk, thr_u | bit, thr_u)

    thr_u = lax.fori_loop(0, 32, value_step, jnp.zeros((1, tq), I32))
    thr = thr_u ^ jnp.int32(INT_MIN)
    thr_ref[...] = thr
    n_gt = count(lambda blk, row0: blk > thr)
    n_ge = count(lambda blk, row0: blk >= jnp.maximum(thr, jnp.int32(INT_MIN + 1)))
    need = topk - n_gt
    j_ref[...] = jnp.where(thr == jnp.int32(INT_MIN), -1, jnp.int32(2 ** n_index_bits))

    @pl.when(jnp.max(n_ge) > topk)
    def _():
        def index_step(it, j):
            cand = j | jnp.left_shift(jnp.int32(1), n_index_bits - 1 - it)

            def pred(blk, row0):
                sidx = row0 + lax.broadcasted_iota(I32, blk.shape, 0)
                return (blk == thr) & (sidx < cand)
            return jnp.where(count(pred) < need, cand, j)
        j = lax.fori_loop(0, n_index_bits, index_step, jnp.zeros((1, tq), I32))
        j_ref[...] = jnp.where(thr == jnp.int32(INT_MIN), -1, j)


def _pattn_kernel(qi_ref, ki_ref, wit_ref, q_ref, k_ref, vt_ref, o_ref,
                  key_ref, dist_ref, lg_ref, acc_ref, thr_ref, j_ref, *, tq, dh, topk, n_index_bits, slopes):
    i = pl.program_id(1)
    n_chunks = i + 1
    kc = tq
    t0 = i * tq
    col = lax.broadcasted_iota(I32, (kc, tq), 1)
    row = lax.broadcasted_iota(I32, (kc, tq), 0)

    qis = [qi_ref[:, h * IDX_DIM:(h + 1) * IDX_DIM] for h in range(N_IDX_HEADS)]

    def score_chunk(c, carry):
        r0 = pl.multiple_of(c * kc, kc)
        kic = ki_ref[pl.ds(r0, kc), 0:IDX_DIM]
        acc = jnp.zeros((kc, tq), F32)
        for h in range(N_IDX_HEADS):
            s = _nt_dot(kic, qis[h])
            acc = acc + jnp.maximum(s, 0.0) * wit_ref[h:h + 1, :]
        causal = (row + c * kc) <= (col + t0)
        key_ref[pl.ds(r0, kc), :] = jnp.where(causal, _sortable(acc), jnp.int32(INT_MIN))
        return carry

    lax.fori_loop(0, n_chunks, score_chunk, 0)
    _topk_threshold(key_ref, n_chunks, kc, topk, n_index_bits, thr_ref, j_ref)
    thr = thr_ref[...]
    jj = j_ref[...]

    def dist_chunk(c, carry):
        r0 = pl.multiple_of(c * kc, kc)
        key = key_ref[pl.ds(r0, kc), :]
        sidx = row + c * kc
        sel = (key > thr) | ((key == thr) & (sidx <= jj))
        dist = ((col + t0) - sidx).astype(F32)
        dist_ref[pl.ds(r0, kc), :] = jnp.where(sel, dist, MASKED_DIST)
        return carry

    lax.fori_loop(0, n_chunks, dist_chunk, 0)

    group = N_HEADS // N_KV_HEADS
    heads = range(N_HEADS)
    qhs = [q_ref[:, h * dh:(h + 1) * dh] for h in heads]

    def fold8(x, op):
        return op(x.reshape(kc // SUBLANES, SUBLANES, tq), axis=0)

    def logit_chunk(c, ms):
        r0 = pl.multiple_of(c * kc, kc)
        dist = dist_ref[pl.ds(r0, kc), :]
        out = []
        for h in heads:
            v = h // group
            s = _nt_dot(k_ref[pl.ds(r0, kc), v * dh:(v + 1) * dh], qhs[h])
            lg = s - slopes[h] * dist
            lg_ref[h, pl.ds(r0, kc), :] = lg
            out.append(jnp.maximum(ms[h], fold8(lg, jnp.max)))
        return tuple(out)

    m8 = lax.fori_loop(0, n_chunks, logit_chunk,
                       tuple(jnp.full((SUBLANES, tq), -jnp.inf, F32) for _ in heads))
    mx = [jnp.max(m, axis=0, keepdims=True) for m in m8]
    acc_ref[...] = jnp.zeros(acc_ref.shape, F32)

    def pv_chunk(c, ls):
        r0 = pl.multiple_of(c * kc, kc)
        out = []
        for h in heads:
            v = h // group
            e = jnp.exp(lg_ref[h, pl.ds(r0, kc), :] - mx[h])
            out.append(ls[h] + fold8(e, jnp.sum))
            acc_ref[h] += jnp.dot(vt_ref[c, v * dh:(v + 1) * dh, :], e.astype(BF16),
                                  preferred_element_type=F32)
        return tuple(out)

    l8 = lax.fori_loop(0, n_chunks, pv_chunk, tuple(jnp.zeros((SUBLANES, tq), F32) for _ in heads))
    for h in heads:
        inv = 1.0 / jnp.sum(l8[h], axis=0, keepdims=True)
        o_ref[:, h * dh:(h + 1) * dh] = (acc_ref[h] * inv).T.astype(BF16)


def _pattn(qi, kib, wit, q, kb, vt, *, batch, seq, tq, topk):
    m, d = q.shape
    dh = d // N_HEADS
    dkv = N_KV_HEADS * dh
    dqi = qi.shape[1]
    nq = seq // tq
    slopes = tuple(float(2.0 ** (-8.0 * (h + 1) / N_HEADS)) for h in range(N_HEADS))
    n_index_bits = max(1, int(math.ceil(math.log2(seq))))
    qrow = lambda n: pl.BlockSpec((tq, n), lambda b, i: (b * nq + i, 0))
    return pl.pallas_call(
        functools.partial(_pattn_kernel, tq=tq, dh=dh, topk=topk, n_index_bits=n_index_bits, slopes=slopes),
        grid=(batch, nq),
        in_specs=[qrow(dqi),
                  pl.BlockSpec((seq, LANES), lambda b, i: (b, 0)),
                  pl.BlockSpec((N_IDX_HEADS, tq), lambda b, i: (0, b * nq + i)),
                  qrow(d),
                  pl.BlockSpec((seq, dkv), lambda b, i: (b, 0)),
                  pl.BlockSpec((None, nq, dkv, tq), lambda b, i: (b, 0, 0, 0))],
        out_specs=qrow(d),
        out_shape=jax.ShapeDtypeStruct((m, d), BF16),
        scratch_shapes=[pltpu.VMEM((seq, tq), I32), pltpu.VMEM((seq, tq), F32),
                        pltpu.VMEM((N_HEADS, seq, tq), F32), pltpu.VMEM((N_HEADS, dh, tq), F32),
                        pltpu.VMEM((1, tq), I32), pltpu.VMEM((1, tq), I32)],
        compiler_params=_cparams(("parallel", "arbitrary")),
        name="pattn",
    )(qi, kib, wit, q, kb, vt)


def _sidx_kernel(pt_ref, qi_ref, w_ref, kin_ref, *rest, n_pages, page, t_new, lp):
    pages = rest[:n_pages]
    key_ref = rest[n_pages]
    kall_ref = rest[n_pages + 1]
    past = n_pages * page
    for p in range(n_pages):
        kall_ref[:, p * page:(p + 1) * page] = pages[p][...].astype(BF16)
    kall_ref[:, past:lp] = jnp.zeros((IDX_DIM, lp - past), BF16)
    kall_ref[:, past:past + t_new] = kin_ref[...]
    s = jnp.dot(qi_ref[...], kall_ref[...], preferred_element_type=F32)
    sw = jnp.maximum(s, 0.0) * w_ref[...]
    isc = jnp.sum(sw.reshape(t_new, N_IDX_HEADS, lp), axis=1)
    sidx = lax.broadcasted_iota(I32, (t_new, lp), 1)
    t = lax.broadcasted_iota(I32, (t_new, lp), 0)
    causal = sidx <= past + t
    key_ref[...] = jnp.where(causal, _sortable(isc), jnp.int32(INT_MIN))


def _sidx(page_table, qi32, w32, kib_new_t, cache_idx_k_t, *, lp):
    bd, n_pages = page_table.shape
    page = cache_idx_k_t.shape[2]
    t_new = kib_new_t.shape[2]
    rows = qi32.shape[1]
    page_spec = lambda p: pl.BlockSpec((None, IDX_DIM, page), lambda b, pt, p=p: (pt[b, p], 0, 0))
    grid_spec = pltpu.PrefetchScalarGridSpec(
        num_scalar_prefetch=1,
        grid=(bd,),
        in_specs=[pl.BlockSpec((None, rows, IDX_DIM), lambda b, pt: (b, 0, 0)),
                  pl.BlockSpec((None, rows, 1), lambda b, pt: (b, 0, 0)),
                  pl.BlockSpec((None, IDX_DIM, t_new), lambda b, pt: (b, 0, 0))]
        + [page_spec(p) for p in range(n_pages)],
        out_specs=pl.BlockSpec((None, t_new, lp), lambda b, pt: (b, 0, 0)),
        scratch_shapes=[pltpu.VMEM((IDX_DIM, lp), BF16)],
    )
    return pl.pallas_call(
        functools.partial(_sidx_kernel, n_pages=n_pages, page=page, t_new=t_new, lp=lp),
        grid_spec=grid_spec,
        out_shape=jax.ShapeDtypeStruct((bd, t_new, lp), I32),
        compiler_params=_cparams(("arbitrary",)),
        name="sidx",
    )(page_table, qi32, w32, kib_new_t, *([cache_idx_k_t] * n_pages))


def _sthr_kernel(key_ref, thr_ref, j_ref, *, kc, topk, n_index_bits):
    _topk_threshold(key_ref, key_ref.shape[0] // kc, kc, topk, n_index_bits, thr_ref, j_ref)


def _sthr(keys_t, *, tq, kc, topk):
    lp, nq = keys_t.shape
    n_index_bits = max(1, int(math.ceil(math.log2(lp))))
    return pl.pallas_call(
        functools.partial(_sthr_kernel, kc=kc, topk=topk, n_index_bits=n_index_bits),
        grid=(nq // tq,),
        in_specs=[pl.BlockSpec((lp, tq), lambda i: (0, i))],
        out_specs=[pl.BlockSpec((1, tq), lambda i: (0, i))] * 2,
        out_shape=[jax.ShapeDtypeStruct((1, nq), I32)] * 2,
        compiler_params=_cparams(("parallel",)),
        name="sthr",
    )(keys_t)


def _sattn_kernel(pt_ref, q_ref, key_ref, thr_ref, j_ref, tpos_ref, slope_ref, kn_ref, vn_ref, *rest,
                  n_pages, page, t_new, lp, dh):
    kpages = rest[:n_pages]
    vpages = rest[n_pages:2 * n_pages]
    o_ref = rest[2 * n_pages]
    kall_ref, vall_ref = rest[2 * n_pages + 1:]
    past = n_pages * page
    for v in range(N_KV_HEADS):
        for p in range(n_pages):
            kall_ref[v, p * page:(p + 1) * page, :] = kpages[p][pl.ds(v, page, stride=N_KV_HEADS), :].astype(BF16)
            vall_ref[v, p * page:(p + 1) * page, :] = vpages[p][pl.ds(v, page, stride=N_KV_HEADS), :].astype(BF16)
        kall_ref[v, past:lp, :] = jnp.zeros((lp - past, dh), BF16)
        vall_ref[v, past:lp, :] = jnp.zeros((lp - past, dh), BF16)
        kall_ref[v, past:past + t_new, :] = kn_ref[:, v * dh:(v + 1) * dh]
        vall_ref[v, past:past + t_new, :] = vn_ref[:, v * dh:(v + 1) * dh]

    key = key_ref[...]
    thr = thr_ref[...]
    sidx = lax.broadcasted_iota(I32, key.shape, 1)
    sel = (key > thr) | ((key == thr) & (sidx <= j_ref[...]))
    dist = jnp.where(sel, (tpos_ref[...] - sidx).astype(F32), MASKED_DIST)
    for v in range(N_KV_HEADS):
        s = _nt_dot(q_ref[v], kall_ref[v])
        lg = s - slope_ref[v] * dist
        m = jnp.max(lg, axis=-1, keepdims=True)
        e = jnp.exp(lg - m)
        l = jnp.sum(e, axis=-1, keepdims=True)
        o = jnp.dot(e.astype(BF16), vall_ref[v], preferred_element_type=F32)
        o_ref[v] = (o / l).astype(BF16)


def _sattn(page_table, q16, keys16, thr16, j16, tpos16, slope16, kb_new, vb_new, cache_k, cache_v, *, lp):
    bd, n_pages = page_table.shape
    dh = cache_k.shape[2]
    page = cache_k.shape[1] // N_KV_HEADS
    dkv = N_KV_HEADS * dh
    t_new = kb_new.shape[1]
    rows = q16.shape[2]
    per_seq = lambda *s: pl.BlockSpec((None,) + s, lambda b, pt: (b,) + (0,) * len(s))
    page_spec = lambda p: pl.BlockSpec((None, page * N_KV_HEADS, dh), lambda b, pt, p=p: (pt[b, p], 0, 0))
    grid_spec = pltpu.PrefetchScalarGridSpec(
        num_scalar_prefetch=1,
        grid=(bd,),
        in_specs=[per_seq(N_KV_HEADS, rows, dh), per_seq(rows, lp), per_seq(rows, 1), per_seq(rows, 1),
                  pl.BlockSpec((rows, 1), lambda b, pt: (0, 0)),
                  pl.BlockSpec((N_KV_HEADS, rows, 1), lambda b, pt: (0, 0, 0)),
                  per_seq(t_new, dkv), per_seq(t_new, dkv)]
        + [page_spec(p) for p in range(n_pages)] * 2,
        out_specs=per_seq(N_KV_HEADS, rows, dh),
        scratch_shapes=[pltpu.VMEM((N_KV_HEADS, lp, dh), BF16), pltpu.VMEM((N_KV_HEADS, lp, dh), BF16)],
    )
    return pl.pallas_call(
        functools.partial(_sattn_kernel, n_pages=n_pages, page=page, t_new=t_new, lp=lp, dh=dh),
        grid_spec=grid_spec,
        out_shape=jax.ShapeDtypeStruct((bd, N_KV_HEADS, rows, dh), BF16),
        compiler_params=_cparams(("arbitrary",)),
        name="sattn",
    )(page_table, q16, keys16, thr16, j16, tpos16, slope16, kb_new, vb_new,
      *([cache_k] * n_pages), *([cache_v] * n_pages))


def _ln_swish(y, g, b):
    yc = y - jnp.mean(y, axis=-1, keepdims=True)
    yn = yc * lax.rsqrt(jnp.mean(yc * yc, axis=-1, keepdims=True) + EPS) * g + b
    return yn * jax.nn.sigmoid(yn)


def _pconv_kernel(z_ref, halo_ref, w_ref, b_ref, g_ref, beta_ref, o_ref, zs_ref, sh_ref, y_ref, *, tm, halo):
    d = z_ref.shape[1]
    n = halo + tm
    first = pl.program_id(1) == 0
    zs_ref[0:halo, :] = jnp.where(first, 0.0, halo_ref[...])
    zs_ref[halo:n, :] = z_ref[...]
    zs_ref[n:n + SUBLANES, :] = jnp.zeros((SUBLANES, d), F32)
    for b in range(SUBLANES):
        sh_ref[b] = zs_ref[b:b + n, :]
    lead = halo - (CONV_WIDTH - 1)
    rows = 64
    for lc in range(d // LANES):
        ls = slice(lc * LANES, (lc + 1) * LANES)

        def row_chunk(rc, carry, ls=ls):
            r0 = pl.multiple_of(rc * rows, rows)
            n_acc = 4
            accs = [jnp.zeros((rows, LANES), F32) + b_ref[:, ls]] + [jnp.zeros((rows, LANES), F32)] * (n_acc - 1)
            for j in range(CONV_WIDTH):
                off = lead + j
                a = off - off % SUBLANES
                accs[j % n_acc] = accs[j % n_acc] + (w_ref[j:j + 1, ls]
                                                     * sh_ref[off % SUBLANES, pl.ds(r0 + a, rows), ls])
            y_ref[pl.ds(r0, rows), ls] = (accs[0] + accs[1]) + (accs[2] + accs[3])
            return carry

        lax.fori_loop(0, tm // rows, row_chunk, 0)
    o_ref[...] = _ln_swish(y_ref[...], g_ref[...], beta_ref[...]).astype(BF16)


def _pconv(z, w_dw, b_dw, ln_g, ln_b, *, batch, seq, tm):
    m, d = z.shape
    halo = 32
    nt = seq // tm
    hb = tm // halo
    const = lambda s: pl.BlockSpec(s, lambda b, i: (0,) * len(s))
    return pl.pallas_call(
        functools.partial(_pconv_kernel, tm=tm, halo=halo),
        grid=(batch, nt),
        in_specs=[pl.BlockSpec((tm, d), lambda b, i: (b * nt + i, 0)),
                  pl.BlockSpec((halo, d), lambda b, i: (jnp.maximum((b * nt + i) * hb - 1, 0), 0)),
                  const((CONV_WIDTH, d)), const((1, d)), const((1, d)), const((1, d))],
        out_specs=pl.BlockSpec((tm, d), lambda b, i: (b * nt + i, 0)),
        out_shape=jax.ShapeDtypeStruct((m, d), BF16),
        scratch_shapes=[pltpu.VMEM((halo + tm + SUBLANES, d), F32), pltpu.VMEM((SUBLANES, halo + tm, d), F32),
                        pltpu.VMEM((tm, d), F32)],
        compiler_params=_cparams(("parallel", "arbitrary")),
        name="pconv",
    )(z, z, w_dw, b_dw.reshape(1, d), ln_g.reshape(1, d), ln_b.reshape(1, d))


def _sconv_kernel(z_ref, st_ref, w_ref, b_ref, g_ref, beta_ref, o_ref, so_ref, *, t_new):
    left = CONV_WIDTH - 1

    def zp(r):
        return st_ref[:, r, :] if r < left else z_ref[:, r - left, :]

    for t in range(t_new):
        acc = jnp.zeros(zp(0).shape, F32) + b_ref[...]
        for j in range(CONV_WIDTH):
            acc = acc + w_ref[j:j + 1, :] * zp(t + j)
        o_ref[:, t, :] = _ln_swish(acc, g_ref[...], beta_ref[...])
    for r in range(left):
        so_ref[:, r, :] = zp(r + t_new)


def _sconv(z3, state, w_dw, b_dw, ln_g, ln_b, *, bs):
    bd, t_new, d = z3.shape
    left = state.shape[1]
    const = lambda s: pl.BlockSpec(s, lambda i: (0,) * len(s))
    return pl.pallas_call(
        functools.partial(_sconv_kernel, t_new=t_new),
        grid=(bd // bs,),
        in_specs=[pl.BlockSpec((bs, t_new, d), lambda i: (i, 0, 0)),
                  pl.BlockSpec((bs, left, d), lambda i: (i, 0, 0)),
                  const((CONV_WIDTH, d)), const((1, d)), const((1, d)), const((1, d))],
        out_specs=[pl.BlockSpec((bs, t_new, d), lambda i: (i, 0, 0)),
                   pl.BlockSpec((bs, left, d), lambda i: (i, 0, 0))],
        out_shape=[jax.ShapeDtypeStruct((bd, t_new, d), F32), jax.ShapeDtypeStruct((bd, left, d), F32)],
        compiler_params=_cparams(("parallel",)),
        name="sconv",
    )(z3, state, w_dw, b_dw.reshape(1, d), ln_g.reshape(1, d), ln_b.reshape(1, d))


def _mid_kernel(x_ref, attn_ref, cv_ref, ga_ref, gb_ref, gate1_ref, shift2_ref, scale2_ref, g2_ref,
                wo_ref, wc_ref, wout_ref, wq_ref, x1_ref, h2_ref, qp_ref):
    ao = jnp.dot(attn_ref[...], wo_ref[...], preferred_element_type=F32)
    co = jnp.dot(cv_ref[...].astype(BF16), wc_ref[...], preferred_element_type=F32)
    merged = ga_ref[...] * ao + gb_ref[...] * co
    x1 = x_ref[...] + gate1_ref[...] * jnp.dot(merged.astype(BF16), wout_ref[...], preferred_element_type=F32)
    x1_ref[...] = x1
    xn = x1 * lax.rsqrt(jnp.mean(x1 * x1, axis=-1, keepdims=True) + EPS) * g2_ref[...]
    h2 = (xn * (1.0 + scale2_ref[...]) + shift2_ref[...]).astype(BF16)
    h2_ref[...] = h2
    qp_ref[...] = jnp.dot(h2, wq_ref[...], preferred_element_type=F32)


def _mid(x, attn, cv, ga, gb, mod3, g2, wo, wc, wout, wq, *, tm, tiles_per_group):
    m, d = x.shape
    nq = wq.shape[1]
    row = lambda n: pl.BlockSpec((tm, n), lambda i: (i, 0))
    const = lambda s: pl.BlockSpec(s, lambda i: (0,) * len(s))
    mod = lambda c: _mod_spec(mod3, tm, d, tiles_per_group, c)
    return pl.pallas_call(
        _mid_kernel,
        grid=(m // tm,),
        in_specs=[row(d), row(d), row(d), row(d), row(d), mod(2), mod(3), mod(4), const((1, d)),
                  const((d, d)), const((d, d)), const((d, d)), const((d, nq))],
        out_specs=[row(d), row(d), row(nq)],
        out_shape=[jax.ShapeDtypeStruct((m, d), F32), jax.ShapeDtypeStruct((m, d), BF16),
                   jax.ShapeDtypeStruct((m, nq), F32)],
        compiler_params=_cparams(("parallel",)),
        name="mid",
    )(x, attn, cv, ga, gb, mod3, mod3, mod3, g2, wo, wc, wout, wq)


def _extract_top(xs, count, out_refs):
    def step(r, xs):
        nxt = []
        for x, out_ref in zip(xs, out_refs):
            rows, lanes = x.shape
            m = jnp.max(jnp.max(x.reshape(rows // SUBLANES, SUBLANES, lanes), axis=0), axis=0, keepdims=True)
            out_ref[pl.ds(r, 1), :] = m
            nxt.append(jnp.where(x == m, -jnp.inf, x))
        return tuple(nxt)

    lax.fori_loop(0, count, step, tuple(xs))


def _merge_exchange_pairs(n):
    t = n.bit_length() - 1
    pairs = []
    p = 1 << (t - 1)
    while p > 0:
        q, r, d = 1 << (t - 1), 0, p
        while d > 0:
            pairs += [(i, i + d) for i in range(n - d) if i & p == r]
            d, q, r = q - p, q >> 1, p
        p >>= 1
    return pairs


def _sorted_top_sublane_tiles(x, count):
    n = count
    v = [x[SUBLANES * j:SUBLANES * (j + 1), :] for j in range(n)]
    for i, j in _merge_exchange_pairs(n):
        v[i], v[j] = jnp.maximum(v[i], v[j]), jnp.minimum(v[i], v[j])
    shift = SUBLANES // 2
    while shift >= 1:
        v = [jnp.maximum(v[j], pltpu.roll(v[n - 1 - j], shift, 0)) for j in range(n)]
        k = n // 2
        while k >= 1:
            for i in range(n):
                if i & k == 0:
                    v[i], v[i + k] = jnp.maximum(v[i], v[i + k]), jnp.minimum(v[i], v[i + k])
            k //= 2
        shift //= 2
    return v


def _gelu_tanh(x):
    return (0.5 * x) * (1.0 + jnp.tanh(x * (0.7978845608028654 + 0.035677408136300125 * (x * x))))


def _peer_kernel(h2_ref, qp_ref, sk1_ref, sk2_ref, u_ref, vt_ref, x1_ref, gate2_ref, y_ref,
                 n1_ref, e1_ref, r2_ref, e2_ref, acc_ref, work_ref, cand_ref, topc_ref,
                 *part_refs, tm, te, half, pairs, parts):
    e_step = pl.program_id(1)
    nk = N_SUBKEYS
    at_refs = part_refs[:parts]
    ht_refs = part_refs[parts:]

    @pl.when(e_step == 0)
    def _():
        acc_ref[...] = jnp.zeros(acc_ref.shape, F32)
        for h in range(PEER_HEADS):
            base = h * 2 * half
            q1 = qp_ref[:, base:base + half].astype(BF16)
            q2 = qp_ref[:, base + half:base + 2 * half].astype(BF16)
            work_ref[0] = _nt_dot(sk1_ref[...], q1)
            work_ref[1] = _nt_dot(sk2_ref[...], q2)
            for lc in range(tm // LANES):
                ls = slice(lc * LANES, (lc + 1) * LANES)
                s1 = work_ref[0, :, ls]
                s2 = work_ref[1, :, ls]
                top1 = _sorted_top_sublane_tiles(s1, PEER_TOPK)
                top2 = _sorted_top_sublane_tiles(s2, PEER_TOPK)
                cand_ref[...] = jnp.full(cand_ref.shape, -jnp.inf, F32)
                for n, (a, b) in enumerate(pairs):
                    cand_ref[n:n + 1, :] = top1[a][0:1, :] + top2[b][0:1, :]
                _extract_top((cand_ref[...],), PEER_TOPK, (topc_ref,))
                topc = topc_ref[...]
                best = topc[0:1, :]
                norm = jnp.sum(jnp.exp(topc - best), axis=0, keepdims=True)
                thr = jnp.broadcast_to(topc[PEER_TOPK - 1:PEER_TOPK, :], (SUBLANES, LANES))
                tiles = (nk // SUBLANES, SUBLANES, LANES)
                s1t = s1.reshape(tiles)
                s2t = s2.reshape(tiles)
                n1 = jnp.zeros(tiles, F32)
                r2 = jnp.zeros(tiles, F32)
                for b in range(PEER_TOPK):
                    n1 = jnp.where((s1t + top2[b]) >= thr, float(b + 1), n1)
                    r2 = jnp.where(top2[b] > s2t, float(b + 1), r2)
                n1_ref[h, :, ls] = n1.reshape(nk, LANES)
                r2_ref[h, :, ls] = r2.reshape(nk, LANES).astype(BF16)
                e1 = jnp.where(s1t >= top1[PEER_TOPK - 1], jnp.exp(s1t - top1[0]), 0.0)
                e1_ref[h, :, ls] = e1.reshape(nk, LANES)
                e2 = jnp.exp(s2t - top2[0]) / jnp.broadcast_to(norm, (SUBLANES, LANES))
                e2_ref[h, :, ls] = e2.reshape(nk, LANES).astype(BF16)

    rows = nk // 2
    n_i1 = te // nk
    tp = te // parts

    def pre_act(p):
        at_refs[p][...] = _nt_dot(u_ref[p * tp:(p + 1) * tp, :], h2_ref[...])

    def gate_part(p):
        for jl in range(tp // nk):
            i1 = e_step * n_i1 + p * (tp // nk) + jl
            n1_rows = [n1_ref[h, pl.ds(i1, 1), :] for h in range(PEER_HEADS)]
            w1_rows = [e1_ref[h, pl.ds(i1, 1), :] for h in range(PEER_HEADS)]
            for lc in range(tm // LANES):
                ls = slice(lc * LANES, (lc + 1) * LANES)
                bc = lambda r: jnp.broadcast_to(r[:, ls], (BF16_TILE, LANES)).astype(BF16)[None]
                n1 = [bc(r) for r in n1_rows]
                w1 = [bc(r) for r in w1_rows]
                for sc in range(nk // rows):
                    rs = slice(sc * rows, (sc + 1) * rows)
                    tiles = (rows // BF16_TILE, BF16_TILE, LANES)
                    g = jnp.zeros(tiles, BF16)
                    for h in range(PEER_HEADS):
                        sel = r2_ref[h, rs, ls].reshape(tiles) < n1[h]
                        g = g + jnp.where(sel, e2_ref[h, rs, ls].reshape(tiles) * w1[h], jnp.zeros((), BF16))
                    r0 = jl * nk + sc * rows
                    act = _gelu_tanh(at_refs[p][r0:r0 + rows, ls]).astype(BF16)
                    ht_refs[p][r0:r0 + rows, ls] = g.reshape(rows, LANES) * act

    def accumulate(p):
        acc_ref[...] += jnp.dot(vt_ref[:, p * tp:(p + 1) * tp], ht_refs[p][...], preferred_element_type=F32)

    pre_act(0)
    for p in range(parts):
        if p + 1 < parts:
            pre_act(p + 1)
        gate_part(p)
        accumulate(p)

    @pl.when(e_step == pl.num_programs(1) - 1)
    def _():
        y_ref[...] = x1_ref[...] + gate2_ref[...] * acc_ref[...].T


def _peer(h2, qp, sk1, sk2, u, vt, x1, mod3, *, tm, te, tiles_per_group):
    m, d = x1.shape
    nq = qp.shape[1]
    n_exp = u.shape[0]
    half = sk1.shape[1]
    pairs = tuple((a, b) for a in range(PEER_TOPK) for b in range(PEER_TOPK) if (a + 1) * (b + 1) <= PEER_TOPK)
    n_cand = -(-len(pairs) // SUBLANES) * SUBLANES
    row = lambda n: pl.BlockSpec((tm, n), lambda i, e: (i, 0))
    const = lambda s: pl.BlockSpec(s, lambda i, e: (0,) * len(s))
    parts = te // 512
    return pl.pallas_call(
        functools.partial(_peer_kernel, tm=tm, te=te, half=half, pairs=pairs, parts=parts),
        grid=(m // tm, n_exp // te),
        in_specs=[row(d), row(nq), const(sk1.shape), const(sk2.shape),
                  pl.BlockSpec((te, d), lambda i, e: (e, 0)),
                  pl.BlockSpec((d, te), lambda i, e: (0, e)),
                  row(d),
                  _mod_spec(mod3, tm, d, tiles_per_group, 5)],
        out_specs=row(d),
        out_shape=jax.ShapeDtypeStruct((m, d), F32),
        scratch_shapes=[pltpu.VMEM((PEER_HEADS, N_SUBKEYS, tm), F32)] * 2
        + [pltpu.VMEM((PEER_HEADS, N_SUBKEYS, tm), BF16)] * 2
        + [pltpu.VMEM((d, tm), F32),
           pltpu.VMEM((2, N_SUBKEYS, tm), F32),
           pltpu.VMEM((n_cand, LANES), F32), pltpu.VMEM((PEER_TOPK, LANES), F32)]
        + [pltpu.VMEM((te // parts, tm), F32)] * parts + [pltpu.VMEM((te // parts, tm), BF16)] * parts,
        compiler_params=_cparams(("parallel", "arbitrary")),
        name="peer",
    )(h2, qp, sk1, sk2, u, vt, x1, mod3)


def _pack_w_in(w_in, d, dkv, dqi):
    o = 0
    cols = []
    for n in (d, dkv, dkv, dqi):
        cols.append(w_in[:, o:o + n])
        o += n
    kiwi = w_in[:, o:o + IDX_DIM + N_IDX_HEADS]
    o += IDX_DIM + N_IDX_HEADS
    cols.append(jnp.pad(kiwi, ((0, 0), (0, LANES - kiwi.shape[1]))))
    for n in (d, d, d, d):
        cols.append(w_in[:, o:o + n])
        o += n
    return jnp.concatenate(cols, axis=1).astype(BF16)


def kernel(x_prompt, x_sample, cache_k, cache_v, cache_idx_k, state_conv, page_table, c_prompt, c_sample, w_ada, b_ada, g_norm1, g_norm2, w_in, g_q, g_k, w_o_attn, w_dw, b_dw, ln_g, ln_b, w_conv_out, w_out, w_q_peer, sub_keys1, sub_keys2, u_emb, v_emb):
    b, s, d = x_prompt.shape
    bd, t_new, _ = x_sample.shape
    dh = g_q.shape[0]
    dkv = N_KV_HEADS * dh
    dqi = N_IDX_HEADS * IDX_DIM
    n_pages = page_table.shape[1]
    page = cache_k.shape[1]
    past = n_pages * page
    mp, ms = b * s, bd * t_new
    group = N_HEADS // N_KV_HEADS

    tm_p = min(256, s)
    tm_s = min(256, ms)
    tq = min(256, s)
    tm_peer = min(512, ms, mp)
    te = 2048

    w1 = _pack_w_in(w_in, d, dkv, dqi)
    wo = w_o_attn.astype(BF16)
    wc = w_conv_out.astype(BF16)
    wout = w_out.astype(BF16)
    wq = w_q_peer.astype(BF16)
    sk1 = sub_keys1.astype(BF16)
    sk2 = sub_keys2.astype(BF16)
    u = u_emb.astype(BF16)
    vt = v_emb.astype(BF16).T
    g1 = g_norm1.reshape(1, d)
    g2 = g_norm2.reshape(1, d)
    gq = g_q.reshape(1, dh)
    gk = g_k.reshape(1, dh)

    mod = _ada(jnp.concatenate([c_prompt, c_sample], axis=0), w_ada, b_ada)
    mod_p = mod[:b].reshape(b, 1, N_ADA * d)
    mod_s = jnp.repeat(mod[b:], t_new, axis=0)

    xp = x_prompt.reshape(mp, d)
    q, k, kb, v, vb, qi, kiwi, kib, z, ga, gb = _inproj(xp, mod_p, g1, gq, gk, w1, tm=tm_p,
                                                        tiles_per_group=s // tm_p)
    wit = kiwi[:, IDX_DIM:IDX_DIM + N_IDX_HEADS].T
    vbt = jnp.swapaxes(vb.reshape(b, s // tq, tq, dkv), 2, 3)
    attn = _pattn(qi, kib, wit, q, kb, vbt, batch=b, seq=s, tq=tq, topk=min(TOPK_MAX, s // 4))
    cv = _pconv(z, w_dw, b_dw, ln_g, ln_b, batch=b, seq=s, tm=tm_p)
    x1, h2, qp = _mid(xp, attn, cv, ga, gb, mod_p, g2, wo, wc, wout, wq, tm=tm_p, tiles_per_group=s // tm_p)
    y_prompt = _peer(h2, qp, sk1, sk2, u, vt, x1, mod_p, tm=tm_peer, te=te, tiles_per_group=s // tm_peer)
    k_prompt = k.reshape(b, s, N_KV_HEADS, dh)
    v_prompt = v.reshape(b, s, N_KV_HEADS, dh)
    idx_k_prompt = kiwi[:, :IDX_DIM].reshape(b, s, IDX_DIM)
    conv_prompt = z.reshape(b, s, d)[:, s - (CONV_WIDTH - 1):]

    xs = x_sample.reshape(ms, d)
    q, k, kb, v, vb, qi, kiwi, kib, z, ga, gb = _inproj(xs, mod_s, g1, gq, gk, w1, tm=tm_s, tiles_per_group=1)
    lp = -(-(past + t_new) // LANES) * LANES
    qi32 = qi.reshape(bd, t_new * N_IDX_HEADS, IDX_DIM)
    w32 = kiwi[:, IDX_DIM:IDX_DIM + N_IDX_HEADS].reshape(bd, t_new * N_IDX_HEADS, 1)
    kib_new_t = jnp.swapaxes(kib[:, :IDX_DIM].reshape(bd, t_new, IDX_DIM), 1, 2)
    keys = _sidx(page_table, qi32, w32, kib_new_t, jnp.swapaxes(cache_idx_k, 1, 2), lp=lp)
    keys_t = keys.reshape(ms, lp).T
    thr, jj = _sthr(keys_t, tq=min(256, ms), kc=LANES, topk=min(TOPK_MAX, (past + t_new) // 4))
    rep = lambda a: jnp.repeat(a.reshape(bd, t_new, -1), group, axis=1)
    q16 = q.reshape(bd, t_new, N_KV_HEADS, group, dh).transpose(0, 2, 1, 3, 4).reshape(bd, N_KV_HEADS,
                                                                                      t_new * group, dh)
    tpos16 = jnp.repeat(past + jnp.arange(t_new, dtype=I32), group).reshape(t_new * group, 1)
    slopes = jnp.exp2(-8.0 * jnp.arange(1, N_HEADS + 1, dtype=F32) / N_HEADS).reshape(N_KV_HEADS, 1, group)
    slope16 = jnp.broadcast_to(slopes, (N_KV_HEADS, t_new, group)).reshape(N_KV_HEADS, t_new * group, 1)
    o16 = _sattn(page_table, q16, rep(keys), rep(thr.reshape(ms, 1)), rep(jj.reshape(ms, 1)), tpos16, slope16,
                 kb.reshape(bd, t_new, dkv), vb.reshape(bd, t_new, dkv),
                 cache_k.reshape(-1, page * N_KV_HEADS, dh), cache_v.reshape(-1, page * N_KV_HEADS, dh), lp=lp)
    attn = o16.reshape(bd, N_KV_HEADS, t_new, group, dh).transpose(0, 2, 1, 3, 4).reshape(ms, d)
    cv3, conv_sample = _sconv(z.reshape(bd, t_new, d), state_conv, w_dw, b_dw, ln_g, ln_b, bs=min(8, bd))
    x1, h2, qp = _mid(xs, attn, cv3.reshape(ms, d), ga, gb, mod_s, g2, wo, wc, wout, wq, tm=tm_s,
                      tiles_per_group=1)
    y_sample = _peer(h2, qp, sk1, sk2, u, vt, x1, mod_s, tm=min(tm_peer, ms), te=te, tiles_per_group=1)

    return (y_prompt.reshape(b, s, d), y_sample.reshape(bd, t_new, d), k_prompt, v_prompt, idx_k_prompt,
            conv_prompt, k.reshape(bd, t_new, N_KV_HEADS, dh), v.reshape(bd, t_new, N_KV_HEADS, dh),
            kiwi[:, :IDX_DIM].reshape(bd, t_new, IDX_DIM), conv_sample)
```

```python
import functools
import math

import jax
import jax.numpy as jnp
import numpy as np
from jax import lax
from jax.experimental import pallas as pl
from jax.experimental.pallas import tpu as pltpu

F32 = jnp.float32
BF16 = jnp.bfloat16
I32 = jnp.int32

N_HEADS = 8
N_KV_HEADS = 2
N_IDX_HEADS = 8
IDX_DIM = 64
TOPK_MAX = 256
CONV_WIDTH = 31
N_SUBKEYS = 128
PEER_HEADS = 8
PEER_TOPK = 16
N_ADA = 6
EPS = 1e-6
MASKED_DIST = 1e33
INT_MIN = -(2 ** 31)
VMEM_LIMIT = 56 * 1024 * 1024
LANES = 128
SUBLANES = 8
BF16_TILE = 16


def _cparams(sem):
    return pltpu.CompilerParams(dimension_semantics=sem, vmem_limit_bytes=VMEM_LIMIT)


def _nt_dot(a, b):
    return lax.dot_general(a, b, (((1,), (1,)), ((), ())), preferred_element_type=F32)


def _mod_spec(mod, tm, d, tiles_per_group, chunk):
    if mod.ndim == 2:
        return pl.BlockSpec((tm, d), lambda i, *_: (i, chunk))
    return pl.BlockSpec((None, mod.shape[1], d), lambda i, *_: (i // tiles_per_group, 0, chunk))


def _sortable(x):
    b = lax.bitcast_convert_type(x, I32)
    return jnp.where(b < 0, b ^ jnp.int32(0x7FFFFFFF), b)


def _ada_kernel(c_ref, w_ref, b_ref, o_ref):
    c = c_ref[...]
    s = c * jax.nn.sigmoid(c)
    o_ref[...] = jnp.dot(s.astype(BF16), w_ref[...].astype(BF16), preferred_element_type=F32) + b_ref[...]


def _ada(c, w_ada, b_ada):
    n, d = c.shape
    nd = w_ada.shape[1]
    return pl.pallas_call(
        _ada_kernel,
        grid=(nd // d,),
        in_specs=[pl.BlockSpec((n, d), lambda j: (0, 0)),
                  pl.BlockSpec((d, d), lambda j: (0, j)),
                  pl.BlockSpec((1, d), lambda j: (0, j))],
        out_specs=pl.BlockSpec((n, d), lambda j: (0, j)),
        out_shape=jax.ShapeDtypeStruct((n, nd), F32),
        compiler_params=_cparams(("arbitrary",)),
        name="ada",
    )(c, w_ada, b_ada.reshape(1, nd))


def _inproj_kernel(x_ref, shift_ref, scale_ref, g1_ref, gq_ref, gk_ref, w_ref,
                   q_ref, k_ref, kb_ref, v_ref, vb_ref, qi_ref, kiwi_ref, kib_ref, z_ref, ga_ref, gb_ref,
                   *, d, dh, dkv, dqi):
    x = x_ref[...]
    xn = x * lax.rsqrt(jnp.mean(x * x, axis=-1, keepdims=True) + EPS) * g1_ref[...]
    hb = (xn * (1.0 + scale_ref[...]) + shift_ref[...]).astype(BF16)

    def mm(lo, n):
        return jnp.dot(hb, w_ref[:, lo:lo + n], preferred_element_type=F32)

    def head_norm(t, g):
        return t * lax.rsqrt(jnp.mean(t * t, axis=-1, keepdims=True) + EPS) * g

    off = 0
    qf = mm(off, d)
    off += d
    for h in range(d // dh):
        qh = head_norm(qf[:, h * dh:(h + 1) * dh], gq_ref[...]) * (dh ** -0.5)
        q_ref[:, h * dh:(h + 1) * dh] = qh.astype(BF16)
    kf = mm(off, dkv)
    off += dkv
    for h in range(dkv // dh):
        kh = head_norm(kf[:, h * dh:(h + 1) * dh], gk_ref[...])
        k_ref[:, h * dh:(h + 1) * dh] = kh
        kb_ref[:, h * dh:(h + 1) * dh] = kh.astype(BF16)
    vf = mm(off, dkv)
    off += dkv
    v_ref[...] = vf
    vb_ref[...] = vf.astype(BF16)
    qi_ref[...] = (mm(off, dqi) * (IDX_DIM ** -0.5)).astype(BF16)
    off += dqi
    kiwi = mm(off, LANES)
    off += LANES
    lane = lax.broadcasted_iota(I32, kiwi.shape, 1)
    is_wi = (lane >= IDX_DIM) & (lane < IDX_DIM + N_IDX_HEADS)
    kiwi_ref[...] = jnp.where(is_wi, kiwi * (N_IDX_HEADS ** -0.5), kiwi)
    kib_ref[...] = kiwi.astype(BF16)
    a = mm(off, d)
    off += d
    g = mm(off, d)
    off += d
    z_ref[...] = a * jax.nn.sigmoid(g)
    ga_ref[...] = jax.nn.sigmoid(mm(off, d))
    off += d
    gb_ref[...] = jax.nn.sigmoid(mm(off, d))


def _inproj(x, mod3, g1, gq, gk, w1, *, tm, tiles_per_group):
    m, d = x.shape
    dh = gq.shape[-1]
    dkv = N_KV_HEADS * dh
    dqi = N_IDX_HEADS * IDX_DIM
    nw = w1.shape[1]
    row = lambda n: pl.BlockSpec((tm, n), lambda i: (i, 0))
    const = lambda s: pl.BlockSpec(s, lambda i: (0,) * len(s))
    mod = lambda c: _mod_spec(mod3, tm, d, tiles_per_group, c)
    shp = lambda n, t: jax.ShapeDtypeStruct((m, n), t)
    return pl.pallas_call(
        functools.partial(_inproj_kernel, d=d, dh=dh, dkv=dkv, dqi=dqi),
        grid=(m // tm,),
        in_specs=[row(d), mod(0), mod(1), const((1, d)), const((1, dh)), const((1, dh)), const((d, nw))],
        out_specs=[row(d), row(dkv), row(dkv), row(dkv), row(dkv), row(dqi), row(LANES), row(LANES),
                   row(d), row(d), row(d)],
        out_shape=[shp(d, BF16), shp(dkv, F32), shp(dkv, BF16), shp(dkv, F32), shp(dkv, BF16), shp(dqi, BF16),
                   shp(LANES, F32), shp(LANES, BF16), shp(d, F32), shp(d, F32), shp(d, F32)],
        compiler_params=_cparams(("parallel",)),
        name="inproj",
    )(x, mod3, mod3, g1, gq, gk, w1)


def _count_rows(pred):
    rows, tq = pred.shape
    return jnp.sum(jnp.where(pred, 1, 0).astype(I32).reshape(rows // SUBLANES, SUBLANES, tq), axis=0)


def _topk_threshold(key_ref, n_chunks, kc, topk, n_index_bits, thr_ref, j_ref):
    tq = key_ref.shape[1]

    def count(pred_fn):
        def body(c, acc):
            blk = key_ref[pl.ds(pl.multiple_of(c * kc, kc), kc), :]
            row0 = c * kc
            return acc + _count_rows(pred_fn(blk, row0))
        acc = lax.fori_loop(0, n_chunks, body, jnp.zeros((SUBLANES, tq), I32))
        return jnp.sum(acc, axis=0, keepdims=True)

    def value_step(it, thr_u):
        bit = jnp.left_shift(jnp.int32(1), 31 - it)
        cand_s = (thr_u | bit) ^ jnp.int32(INT_MIN)
        cnt = count(lambda blk, row0: blk >= cand_s)
        return jnp.where(cnt >= topk, thr_u | bit, thr_u)

    thr_u = lax.fori_loop(0, 32, value_step, jnp.zeros((1, tq), I32))
    thr = thr_u ^ jnp.int32(INT_MIN)
    thr_ref[...] = thr
    n_gt = count(lambda blk, row0: blk > thr)
    n_ge = count(lambda blk, row0: blk >= jnp.maximum(thr, jnp.int32(INT_MIN + 1)))
    need = topk - n_gt
    j_ref[...] = jnp.where(thr == jnp.int32(INT_MIN), -1, jnp.int32(2 ** n_index_bits))

    @pl.when(jnp.max(n_ge) > topk)
    def _():
        def index_step(it, j):
            cand = j | jnp.left_shift(jnp.int32(1), n_index_bits - 1 - it)

            def pred(blk, row0):
                sidx = row0 + lax.broadcasted_iota(I32, blk.shape, 0)
                return (blk == thr) & (sidx < cand)
            return jnp.where(count(pred) < need, cand, j)
        j = lax.fori_loop(0, n_index_bits, index_step, jnp.zeros((1, tq), I32))
        j_ref[...] = jnp.where(thr == jnp.int32(INT_MIN), -1, j)


def _pattn_kernel(qi_ref, ki_ref, wit_ref, q_ref, k_ref, vt_ref, o_ref,
                  key_ref, dist_ref, lg_ref, acc_ref, thr_ref, j_ref, *, tq, dh, topk, n_index_bits, slopes):
    i = pl.program_id(1)
    n_chunks = i + 1
    kc = tq
    t0 = i * tq
    col = lax.broadcasted_iota(I32, (kc, tq), 1)
    row = lax.broadcasted_iota(I32, (kc, tq), 0)

    qis = [qi_ref[:, h * IDX_DIM:(h + 1) * IDX_DIM] for h in range(N_IDX_HEADS)]

    def score_chunk(c, carry):
        r0 = pl.multiple_of(c * kc, kc)
        kic = ki_ref[pl.ds(r0, kc), 0:IDX_DIM]
        acc = jnp.zeros((kc, tq), F32)
        for h in range(N_IDX_HEADS):
            s = _nt_dot(kic, qis[h])
            acc = acc + jnp.maximum(s, 0.0) * wit_ref[h:h + 1, :]
        causal = (row + c * kc) <= (col + t0)
        key_ref[pl.ds(r0, kc), :] = jnp.where(causal, _sortable(acc), jnp.int32(INT_MIN))
        return carry

    lax.fori_loop(0, n_chunks, score_chunk, 0)
    _topk_threshold(key_ref, n_chunks, kc, topk, n_index_bits, thr_ref, j_ref)
    thr = thr_ref[...]
    jj = j_ref[...]

    def dist_chunk(c, carry):
        r0 = pl.multiple_of(c * kc, kc)
        key = key_ref[pl.ds(r0, kc), :]
        sidx = row + c * kc
        sel = (key > thr) | ((key == thr) & (sidx <= jj))
        dist = ((col + t0) - sidx).astype(F32)
        dist_ref[pl.ds(r0, kc), :] = jnp.where(sel, dist, MASKED_DIST)
        return carry

    lax.fori_loop(0, n_chunks, dist_chunk, 0)

    group = N_HEADS // N_KV_HEADS
    heads = range(N_HEADS)
    qhs = [q_ref[:, h * dh:(h + 1) * dh] for h in heads]

    def fold8(x, op):
        return op(x.reshape(kc // SUBLANES, SUBLANES, tq), axis=0)

    def logit_chunk(c, ms):
        r0 = pl.multiple_of(c * kc, kc)
        dist = dist_ref[pl.ds(r0, kc), :]
        out = []
        for h in heads:
            v = h // group
            s = _nt_dot(k_ref[pl.ds(r0, kc), v * dh:(v + 1) * dh], qhs[h])
            lg = s - slopes[h] * dist
            lg_ref[h, pl.ds(r0, kc), :] = lg
            out.append(jnp.maximum(ms[h], fold8(lg, jnp.max)))
        return tuple(out)

    m8 = lax.fori_loop(0, n_chunks, logit_chunk,
                       tuple(jnp.full((SUBLANES, tq), -jnp.inf, F32) for _ in heads))
    mx = [jnp.max(m, axis=0, keepdims=True) for m in m8]
    acc_ref[...] = jnp.zeros(acc_ref.shape, F32)

    def pv_chunk(c, ls):
        r0 = pl.multiple_of(c * kc, kc)
        out = []
        for h in heads:
            v = h // group
            e = jnp.exp(lg_ref[h, pl.ds(r0, kc), :] - mx[h])
            out.append(ls[h] + fold8(e, jnp.sum))
            acc_ref[h] += jnp.dot(vt_ref[c, v * dh:(v + 1) * dh, :], e.astype(BF16),
                                  preferred_element_type=F32)
        return tuple(out)

    l8 = lax.fori_loop(0, n_chunks, pv_chunk, tuple(jnp.zeros((SUBLANES, tq), F32) for _ in heads))
    for h in heads:
        inv = 1.0 / jnp.sum(l8[h], axis=0, keepdims=True)
        o_ref[:, h * dh:(h + 1) * dh] = (acc_ref[h] * inv).T.astype(BF16)


def _pattn(qi, kib, wit, q, kb, vt, *, batch, seq, tq, topk):
    m, d = q.shape
    dh = d // N_HEADS
    dkv = N_KV_HEADS * dh
    dqi = qi.shape[1]
    nq = seq // tq
    slopes = tuple(float(2.0 ** (-8.0 * (h + 1) / N_HEADS)) for h in range(N_HEADS))
    n_index_bits = max(1, int(math.ceil(math.log2(seq))))
    qrow = lambda n: pl.BlockSpec((tq, n), lambda b, i: (b * nq + i, 0))
    return pl.pallas_call(
        functools.partial(_pattn_kernel, tq=tq, dh=dh, topk=topk, n_index_bits=n_index_bits, slopes=slopes),
        grid=(batch, nq),
        in_specs=[qrow(dqi),
                  pl.BlockSpec((seq, LANES), lambda b, i: (b, 0)),
                  pl.BlockSpec((N_IDX_HEADS, tq), lambda b, i: (0, b * nq + i)),
                  qrow(d),
                  pl.BlockSpec((seq, dkv), lambda b, i: (b, 0)),
                  pl.BlockSpec((None, nq, dkv, tq), lambda b, i: (b, 0, 0, 0))],
        out_specs=qrow(d),
        out_shape=jax.ShapeDtypeStruct((m, d), BF16),
        scratch_shapes=[pltpu.VMEM((seq, tq), I32), pltpu.VMEM((seq, tq), F32),
                        pltpu.VMEM((N_HEADS, seq, tq), F32), pltpu.VMEM((N_HEADS, dh, tq), F32),
                        pltpu.VMEM((1, tq), I32), pltpu.VMEM((1, tq), I32)],
        compiler_params=_cparams(("parallel", "arbitrary")),
        name="pattn",
    )(qi, kib, wit, q, kb, vt)


def _sidx_kernel(pt_ref, qi_ref, w_ref, kin_ref, *rest, n_pages, page, t_new, lp):
    pages = rest[:n_pages]
    key_ref = rest[n_pages]
    kall_ref = rest[n_pages + 1]
    past = n_pages * page
    for p in range(n_pages):
        kall_ref[:, p * page:(p + 1) * page] = pages[p][...].astype(BF16)
    kall_ref[:, past:lp] = jnp.zeros((IDX_DIM, lp - past), BF16)
    kall_ref[:, past:past + t_new] = kin_ref[...]
    s = jnp.dot(qi_ref[...], kall_ref[...], preferred_element_type=F32)
    sw = jnp.maximum(s, 0.0) * w_ref[...]
    isc = jnp.sum(sw.reshape(t_new, N_IDX_HEADS, lp), axis=1)
    sidx = lax.broadcasted_iota(I32, (t_new, lp), 1)
    t = lax.broadcasted_iota(I32, (t_new, lp), 0)
    causal = sidx <= past + t
    key_ref[...] = jnp.where(causal, _sortable(isc), jnp.int32(INT_MIN))


def _sidx(page_table, qi32, w32, kib_new_t, cache_idx_k_t, *, lp):
    bd, n_pages = page_table.shape
    page = cache_idx_k_t.shape[2]
    t_new = kib_new_t.shape[2]
    rows = qi32.shape[1]
    page_spec = lambda p: pl.BlockSpec((None, IDX_DIM, page), lambda b, pt, p=p: (pt[b, p], 0, 0))
    grid_spec = pltpu.PrefetchScalarGridSpec(
        num_scalar_prefetch=1,
        grid=(bd,),
        in_specs=[pl.BlockSpec((None, rows, IDX_DIM), lambda b, pt: (b, 0, 0)),
                  pl.BlockSpec((None, rows, 1), lambda b, pt: (b, 0, 0)),
                  pl.BlockSpec((None, IDX_DIM, t_new), lambda b, pt: (b, 0, 0))]
        + [page_spec(p) for p in range(n_pages)],
        out_specs=pl.BlockSpec((None, t_new, lp), lambda b, pt: (b, 0, 0)),
        scratch_shapes=[pltpu.VMEM((IDX_DIM, lp), BF16)],
    )
    return pl.pallas_call(
        functools.partial(_sidx_kernel, n_pages=n_pages, page=page, t_new=t_new, lp=lp),
        grid_spec=grid_spec,
        out_shape=jax.ShapeDtypeStruct((bd, t_new, lp), I32),
        compiler_params=_cparams(("arbitrary",)),
        name="sidx",
    )(page_table, qi32, w32, kib_new_t, *([cache_idx_k_t] * n_pages))


def _sthr_kernel(key_ref, thr_ref, j_ref, *, kc, topk, n_index_bits):
    _topk_threshold(key_ref, key_ref.shape[0] // kc, kc, topk, n_index_bits, thr_ref, j_ref)


def _sthr(keys_t, *, tq, kc, topk):
    lp, nq = keys_t.shape
    n_index_bits = max(1, int(math.ceil(math.log2(lp))))
    return pl.pallas_call(
        functools.partial(_sthr_kernel, kc=kc, topk=topk, n_index_bits=n_index_bits),
        grid=(nq // tq,),
        in_specs=[pl.BlockSpec((lp, tq), lambda i: (0, i))],
        out_specs=[pl.BlockSpec((1, tq), lambda i: (0, i))] * 2,
        out_shape=[jax.ShapeDtypeStruct((1, nq), I32)] * 2,
        compiler_params=_cparams(("parallel",)),
        name="sthr",
    )(keys_t)


def _sattn_kernel(pt_ref, q_ref, key_ref, thr_ref, j_ref, tpos_ref, slope_ref, kn_ref, vn_ref, *rest,
                  n_pages, page, t_new, lp, dh):
    kpages = rest[:n_pages]
    vpages = rest[n_pages:2 * n_pages]
    o_ref = rest[2 * n_pages]
    kall_ref, vall_ref = rest[2 * n_pages + 1:]
    past = n_pages * page
    for v in range(N_KV_HEADS):
        for p in range(n_pages):
            kall_ref[v, p * page:(p + 1) * page, :] = kpages[p][pl.ds(v, page, stride=N_KV_HEADS), :].astype(BF16)
            vall_ref[v, p * page:(p + 1) * page, :] = vpages[p][pl.ds(v, page, stride=N_KV_HEADS), :].astype(BF16)
        kall_ref[v, past:lp, :] = jnp.zeros((lp - past, dh), BF16)
        vall_ref[v, past:lp, :] = jnp.zeros((lp - past, dh), BF16)
        kall_ref[v, past:past + t_new, :] = kn_ref[:, v * dh:(v + 1) * dh]
        vall_ref[v, past:past + t_new, :] = vn_ref[:, v * dh:(v + 1) * dh]

    key = key_ref[...]
    thr = thr_ref[...]
    sidx = lax.broadcasted_iota(I32, key.shape, 1)
    sel = (key > thr) | ((key == thr) & (sidx <= j_ref[...]))
    dist = jnp.where(sel, (tpos_ref[...] - sidx).astype(F32), MASKED_DIST)
    for v in range(N_KV_HEADS):
        s = _nt_dot(q_ref[v], kall_ref[v])
        lg = s - slope_ref[v] * dist
        m = jnp.max(lg, axis=-1, keepdims=True)
        e = jnp.exp(lg - m)
        l = jnp.sum(e, axis=-1, keepdims=True)
        o = jnp.dot(e.astype(BF16), vall_ref[v], preferred_element_type=F32)
        o_ref[v] = (o / l).astype(BF16)


def _sattn(page_table, q16, keys16, thr16, j16, tpos16, slope16, kb_new, vb_new, cache_k, cache_v, *, lp):
    bd, n_pages = page_table.shape
    dh = cache_k.shape[2]
    page = cache_k.shape[1] // N_KV_HEADS
    dkv = N_KV_HEADS * dh
    t_new = kb_new.shape[1]
    rows = q16.shape[2]
    per_seq = lambda *s: pl.BlockSpec((None,) + s, lambda b, pt: (b,) + (0,) * len(s))
    page_spec = lambda p: pl.BlockSpec((None, page * N_KV_HEADS, dh), lambda b, pt, p=p: (pt[b, p], 0, 0))
    grid_spec = pltpu.PrefetchScalarGridSpec(
        num_scalar_prefetch=1,
        grid=(bd,),
        in_specs=[per_seq(N_KV_HEADS, rows, dh), per_seq(rows, lp), per_seq(rows, 1), per_seq(rows, 1),
                  pl.BlockSpec((rows, 1), lambda b, pt: (0, 0)),
                  pl.BlockSpec((N_KV_HEADS, rows, 1), lambda b, pt: (0, 0, 0)),
                  per_seq(t_new, dkv), per_seq(t_new, dkv)]
        + [page_spec(p) for p in range(n_pages)] * 2,
        out_specs=per_seq(N_KV_HEADS, rows, dh),
        scratch_shapes=[pltpu.VMEM((N_KV_HEADS, lp, dh), BF16), pltpu.VMEM((N_KV_HEADS, lp, dh), BF16)],
    )
    return pl.pallas_call(
        functools.partial(_sattn_kernel, n_pages=n_pages, page=page, t_new=t_new, lp=lp, dh=dh),
        grid_spec=grid_spec,
        out_shape=jax.ShapeDtypeStruct((bd, N_KV_HEADS, rows, dh), BF16),
        compiler_params=_cparams(("arbitrary",)),
        name="sattn",
    )(page_table, q16, keys16, thr16, j16, tpos16, slope16, kb_new, vb_new,
      *([cache_k] * n_pages), *([cache_v] * n_pages))


def _ln_swish(y, g, b):
    yc = y - jnp.mean(y, axis=-1, keepdims=True)
    yn = yc * lax.rsqrt(jnp.mean(yc * yc, axis=-1, keepdims=True) + EPS) * g + b
    return yn * jax.nn.sigmoid(yn)


def _pconv_kernel(z_ref, halo_ref, w_ref, b_ref, g_ref, beta_ref, o_ref, zs_ref, sh_ref, y_ref, *, tm, halo):
    d = z_ref.shape[1]
    n = halo + tm
    first = pl.program_id(1) == 0
    zs_ref[0:halo, :] = jnp.where(first, 0.0, halo_ref[...])
    zs_ref[halo:n, :] = z_ref[...]
    zs_ref[n:n + SUBLANES, :] = jnp.zeros((SUBLANES, d), F32)
    for b in range(SUBLANES):
        sh_ref[b] = zs_ref[b:b + n, :]
    lead = halo - (CONV_WIDTH - 1)
    rows = 64
    for lc in range(d // LANES):
        ls = slice(lc * LANES, (lc + 1) * LANES)

        def row_chunk(rc, carry, ls=ls):
            r0 = pl.multiple_of(rc * rows, rows)
            n_acc = 4
            accs = [jnp.zeros((rows, LANES), F32) + b_ref[:, ls]] + [jnp.zeros((rows, LANES), F32)] * (n_acc - 1)
            for j in range(CONV_WIDTH):
                off = lead + j
                a = off - off % SUBLANES
                accs[j % n_acc] = accs[j % n_acc] + (w_ref[j:j + 1, ls]
                                                     * sh_ref[off % SUBLANES, pl.ds(r0 + a, rows), ls])
            y_ref[pl.ds(r0, rows), ls] = (accs[0] + accs[1]) + (accs[2] + accs[3])
            return carry

        lax.fori_loop(0, tm // rows, row_chunk, 0)
    o_ref[...] = _ln_swish(y_ref[...], g_ref[...], beta_ref[...]).astype(BF16)


def _pconv(z, w_dw, b_dw, ln_g, ln_b, *, batch, seq, tm):
    m, d = z.shape
    halo = 32
    nt = seq // tm
    hb = tm // halo
    const = lambda s: pl.BlockSpec(s, lambda b, i: (0,) * len(s))
    return pl.pallas_call(
        functools.partial(_pconv_kernel, tm=tm, halo=halo),
        grid=(batch, nt),
        in_specs=[pl.BlockSpec((tm, d), lambda b, i: (b * nt + i, 0)),
                  pl.BlockSpec((halo, d), lambda b, i: (jnp.maximum((b * nt + i) * hb - 1, 0), 0)),
                  const((CONV_WIDTH, d)), const((1, d)), const((1, d)), const((1, d))],
        out_specs=pl.BlockSpec((tm, d), lambda b, i: (b * nt + i, 0)),
        out_shape=jax.ShapeDtypeStruct((m, d), BF16),
        scratch_shapes=[pltpu.VMEM((halo + tm + SUBLANES, d), F32), pltpu.VMEM((SUBLANES, halo + tm, d), F32),
                        pltpu.VMEM((tm, d), F32)],
        compiler_params=_cparams(("parallel", "arbitrary")),
        name="pconv",
    )(z, z, w_dw, b_dw.reshape(1, d), ln_g.reshape(1, d), ln_b.reshape(1, d))


def _sconv_kernel(z_ref, st_ref, w_ref, b_ref, g_ref, beta_ref, o_ref, so_ref, *, t_new):
    left = CONV_WIDTH - 1

    def zp(r):
        return st_ref[:, r, :] if r < left else z_ref[:, r - left, :]

    for t in range(t_new):
        acc = jnp.zeros(zp(0).shape, F32) + b_ref[...]
        for j in range(CONV_WIDTH):
            acc = acc + w_ref[j:j + 1, :] * zp(t + j)
        o_ref[:, t, :] = _ln_swish(acc, g_ref[...], beta_ref[...])
    for r in range(left):
        so_ref[:, r, :] = zp(r + t_new)


def _sconv(z3, state, w_dw, b_dw, ln_g, ln_b, *, bs):
    bd, t_new, d = z3.shape
    left = state.shape[1]
    const = lambda s: pl.BlockSpec(s, lambda i: (0,) * len(s))
    return pl.pallas_call(
        functools.partial(_sconv_kernel, t_new=t_new),
        grid=(bd // bs,),
        in_specs=[pl.BlockSpec((bs, t_new, d), lambda i: (i, 0, 0)),
                  pl.BlockSpec((bs, left, d), lambda i: (i, 0, 0)),
                  const((CONV_WIDTH, d)), const((1, d)), const((1, d)), const((1, d))],
        out_specs=[pl.BlockSpec((bs, t_new, d), lambda i: (i, 0, 0)),
                   pl.BlockSpec((bs, left, d), lambda i: (i, 0, 0))],
        out_shape=[jax.ShapeDtypeStruct((bd, t_new, d), F32), jax.ShapeDtypeStruct((bd, left, d), F32)],
        compiler_params=_cparams(("parallel",)),
        name="sconv",
    )(z3, state, w_dw, b_dw.reshape(1, d), ln_g.reshape(1, d), ln_b.reshape(1, d))


def _mid_kernel(x_ref, attn_ref, cv_ref, ga_ref, gb_ref, gate1_ref, shift2_ref, scale2_ref, g2_ref,
                wo_ref, wc_ref, wout_ref, wq_ref, x1_ref, h2_ref, qp_ref):
    ao = jnp.dot(attn_ref[...], wo_ref[...], preferred_element_type=F32)
    co = jnp.dot(cv_ref[...].astype(BF16), wc_ref[...], preferred_element_type=F32)
    merged = ga_ref[...] * ao + gb_ref[...] * co
    x1 = x_ref[...] + gate1_ref[...] * jnp.dot(merged.astype(BF16), wout_ref[...], preferred_element_type=F32)
    x1_ref[...] = x1
    xn = x1 * lax.rsqrt(jnp.mean(x1 * x1, axis=-1, keepdims=True) + EPS) * g2_ref[...]
    h2 = (xn * (1.0 + scale2_ref[...]) + shift2_ref[...]).astype(BF16)
    h2_ref[...] = h2
    qp_ref[...] = jnp.dot(h2, wq_ref[...], preferred_element_type=F32)


def _mid(x, attn, cv, ga, gb, mod3, g2, wo, wc, wout, wq, *, tm, tiles_per_group):
    m, d = x.shape
    nq = wq.shape[1]
    row = lambda n: pl.BlockSpec((tm, n), lambda i: (i, 0))
    const = lambda s: pl.BlockSpec(s, lambda i: (0,) * len(s))
    mod = lambda c: _mod_spec(mod3, tm, d, tiles_per_group, c)
    return pl.pallas_call(
        _mid_kernel,
        grid=(m // tm,),
        in_specs=[row(d), row(d), row(d), row(d), row(d), mod(2), mod(3), mod(4), const((1, d)),
                  const((d, d)), const((d, d)), const((d, d)), const((d, nq))],
        out_specs=[row(d), row(d), row(nq)],
        out_shape=[jax.ShapeDtypeStruct((m, d), F32), jax.ShapeDtypeStruct((m, d), BF16),
                   jax.ShapeDtypeStruct((m, nq), F32)],
        compiler_params=_cparams(("parallel",)),
        name="mid",
    )(x, attn, cv, ga, gb, mod3, mod3, mod3, g2, wo, wc, wout, wq)


def _extract_top(xs, count, out_refs):
    def step(r, xs):
        nxt = []
        for x, out_ref in zip(xs, out_refs):
            rows, lanes = x.shape
            m = jnp.max(jnp.max(x.reshape(rows // SUBLANES, SUBLANES, lanes), axis=0), axis=0, keepdims=True)
            out_ref[pl.ds(r, 1), :] = m
            nxt.append(jnp.where(x == m, -jnp.inf, x))
        return tuple(nxt)

    lax.fori_loop(0, count, step, tuple(xs))


def _merge_exchange_pairs(n):
    t = n.bit_length() - 1
    pairs = []
    p = 1 << (t - 1)
    while p > 0:
        q, r, d = 1 << (t - 1), 0, p
        while d > 0:
            pairs += [(i, i + d) for i in range(n - d) if i & p == r]
            d, q, r = q - p, q >> 1, p
        p >>= 1
    return pairs


def _sorted_top_sublane_tiles(x, count):
    n = count
    v = [x[SUBLANES * j:SUBLANES * (j + 1), :] for j in range(n)]
    for i, j in _merge_exchange_pairs(n):
        v[i], v[j] = jnp.maximum(v[i], v[j]), jnp.minimum(v[i], v[j])
    shift = SUBLANES // 2
    while shift >= 1:
        v = [jnp.maximum(v[j], pltpu.roll(v[n - 1 - j], shift, 0)) for j in range(n)]
        k = n // 2
        while k >= 1:
            for i in range(n):
                if i & k == 0:
                    v[i], v[i + k] = jnp.maximum(v[i], v[i + k]), jnp.minimum(v[i], v[i + k])
            k //= 2
        shift //= 2
    return v


def _gelu_tanh(x):
    return (0.5 * x) * (1.0 + jnp.tanh(x * (0.7978845608028654 + 0.035677408136300125 * (x * x))))


def _peer_kernel(h2_ref, qp_ref, sk1_ref, sk2_ref, u_ref, vt_ref, x1_ref, gate2_ref, y_ref,
                 n1_ref, e1_ref, r2_ref, e2_ref, acc_ref, work_ref, cand_ref, topc_ref,
                 *part_refs, tm, te, half, pairs, parts):
    e_step = pl.program_id(1)
    nk = N_SUBKEYS
    at_refs = part_refs[:parts]
    ht_refs = part_refs[parts:]

    @pl.when(e_step == 0)
    def _():
        acc_ref[...] = jnp.zeros(acc_ref.shape, F32)
        for h in range(PEER_HEADS):
            base = h * 2 * half
            q1 = qp_ref[:, base:base + half].astype(BF16)
            q2 = qp_ref[:, base + half:base + 2 * half].astype(BF16)
            work_ref[0] = _nt_dot(sk1_ref[...], q1)
            work_ref[1] = _nt_dot(sk2_ref[...], q2)
            for lc in range(tm // LANES):
                ls = slice(lc * LANES, (lc + 1) * LANES)
                s1 = work_ref[0, :, ls]
                s2 = work_ref[1, :, ls]
                top1 = _sorted_top_sublane_tiles(s1, PEER_TOPK)
                top2 = _sorted_top_sublane_tiles(s2, PEER_TOPK)
                cand_ref[...] = jnp.full(cand_ref.shape, -jnp.inf, F32)
                for n, (a, b) in enumerate(pairs):
                    cand_ref[n:n + 1, :] = top1[a][0:1, :] + top2[b][0:1, :]
                _extract_top((cand_ref[...],), PEER_TOPK, (topc_ref,))
                topc = topc_ref[...]
                best = topc[0:1, :]
                norm = jnp.sum(jnp.exp(topc - best), axis=0, keepdims=True)
                thr = jnp.broadcast_to(topc[PEER_TOPK - 1:PEER_TOPK, :], (SUBLANES, LANES))
                tiles = (nk // SUBLANES, SUBLANES, LANES)
                s1t = s1.reshape(tiles)
                s2t = s2.reshape(tiles)
                n1 = jnp.zeros(tiles, F32)
                r2 = jnp.zeros(tiles, F32)
                for b in range(PEER_TOPK):
                    n1 = jnp.where((s1t + top2[b]) >= thr, float(b + 1), n1)
                    r2 = jnp.where(top2[b] > s2t, float(b + 1), r2)
                n1_ref[h, :, ls] = n1.reshape(nk, LANES)
                r2_ref[h, :, ls] = r2.reshape(nk, LANES).astype(BF16)
                e1 = jnp.where(s1t >= top1[PEER_TOPK - 1], jnp.exp(s1t - top1[0]), 0.0)
                e1_ref[h, :, ls] = e1.reshape(nk, LANES)
                e2 = jnp.exp(s2t - top2[0]) / jnp.broadcast_to(norm, (SUBLANES, LANES))
                e2_ref[h, :, ls] = e2.reshape(nk, LANES).astype(BF16)

    rows = nk // 2
    n_i1 = te // nk
    tp = te // parts

    def pre_act(p):
        at_refs[p][...] = _nt_dot(u_ref[p * tp:(p + 1) * tp, :], h2_ref[...])

    def gate_part(p):
        for jl in range(tp // nk):
            i1 = e_step * n_i1 + p * (tp // nk) + jl
            n1_rows = [n1_ref[h, pl.ds(i1, 1), :] for h in range(PEER_HEADS)]
            w1_rows = [e1_ref[h, pl.ds(i1, 1), :] for h in range(PEER_HEADS)]
            for lc in range(tm // LANES):
                ls = slice(lc * LANES, (lc + 1) * LANES)
                bc = lambda r: jnp.broadcast_to(r[:, ls], (BF16_TILE, LANES)).astype(BF16)[None]
                n1 = [bc(r) for r in n1_rows]
                w1 = [bc(r) for r in w1_rows]
                for sc in range(nk // rows):
                    rs = slice(sc * rows, (sc + 1) * rows)
                    tiles = (rows // BF16_TILE, BF16_TILE, LANES)
                    g = jnp.zeros(tiles, BF16)
                    for h in range(PEER_HEADS):
                        sel = r2_ref[h, rs, ls].reshape(tiles) < n1[h]
                        g = g + jnp.where(sel, e2_ref[h, rs, ls].reshape(tiles) * w1[h], jnp.zeros((), BF16))
                    r0 = jl * nk + sc * rows
                    act = _gelu_tanh(at_refs[p][r0:r0 + rows, ls]).astype(BF16)
                    ht_refs[p][r0:r0 + rows, ls] = g.reshape(rows, LANES) * act

    def accumulate(p):
        acc_ref[...] += jnp.dot(vt_ref[:, p * tp:(p + 1) * tp], ht_refs[p][...], preferred_element_type=F32)

    pre_act(0)
    for p in range(parts):
        if p + 1 < parts:
            pre_act(p + 1)
        gate_part(p)
        accumulate(p)

    @pl.when(e_step == pl.num_programs(1) - 1)
    def _():
        y_ref[...] = x1_ref[...] + gate2_ref[...] * acc_ref[...].T


def _peer(h2, qp, sk1, sk2, u, vt, x1, mod3, *, tm, te, tiles_per_group):
    m, d = x1.shape
    nq = qp.shape[1]
    n_exp = u.shape[0]
    half = sk1.shape[1]
    pairs = tuple((a, b) for a in range(PEER_TOPK) for b in range(PEER_TOPK) if (a + 1) * (b + 1) <= PEER_TOPK)
    n_cand = -(-len(pairs) // SUBLANES) * SUBLANES
    row = lambda n: pl.BlockSpec((tm, n), lambda i, e: (i, 0))
    const = lambda s: pl.BlockSpec(s, lambda i, e: (0,) * len(s))
    parts = te // 1024
    return pl.pallas_call(
        functools.partial(_peer_kernel, tm=tm, te=te, half=half, pairs=pairs, parts=parts),
        grid=(m // tm, n_exp // te),
        in_specs=[row(d), row(nq), const(sk1.shape), const(sk2.shape),
                  pl.BlockSpec((te, d), lambda i, e: (e, 0)),
                  pl.BlockSpec((d, te), lambda i, e: (0, e)),
                  row(d),
                  _mod_spec(mod3, tm, d, tiles_per_group, 5)],
        out_specs=row(d),
        out_shape=jax.ShapeDtypeStruct((m, d), F32),
        scratch_shapes=[pltpu.VMEM((PEER_HEADS, N_SUBKEYS, tm), F32)] * 2
        + [pltpu.VMEM((PEER_HEADS, N_SUBKEYS, tm), BF16)] * 2
        + [pltpu.VMEM((d, tm), F32),
           pltpu.VMEM((2, N_SUBKEYS, tm), F32),
           pltpu.VMEM((n_cand, LANES), F32), pltpu.VMEM((PEER_TOPK, LANES), F32)]
        + [pltpu.VMEM((te // parts, tm), F32)] * parts + [pltpu.VMEM((te // parts, tm), BF16)] * parts,
        compiler_params=_cparams(("parallel", "arbitrary")),
        name="peer",
    )(h2, qp, sk1, sk2, u, vt, x1, mod3)


def _pack_w_in(w_in, d, dkv, dqi):
    o = 0
    cols = []
    for n in (d, dkv, dkv, dqi):
        cols.append(w_in[:, o:o + n])
        o += n
    kiwi = w_in[:, o:o + IDX_DIM + N_IDX_HEADS]
    o += IDX_DIM + N_IDX_HEADS
    cols.append(jnp.pad(kiwi, ((0, 0), (0, LANES - kiwi.shape[1]))))
    for n in (d, d, d, d):
        cols.append(w_in[:, o:o + n])
        o += n
    return jnp.concatenate(cols, axis=1).astype(BF16)


def kernel(x_prompt, x_sample, cache_k, cache_v, cache_idx_k, state_conv, page_table, c_prompt, c_sample, w_ada, b_ada, g_norm1, g_norm2, w_in, g_q, g_k, w_o_attn, w_dw, b_dw, ln_g, ln_b, w_conv_out, w_out, w_q_peer, sub_keys1, sub_keys2, u_emb, v_emb):
    b, s, d = x_prompt.shape
    bd, t_new, _ = x_sample.shape
    dh = g_q.shape[0]
    dkv = N_KV_HEADS * dh
    dqi = N_IDX_HEADS * IDX_DIM
    n_pages = page_table.shape[1]
    page = cache_k.shape[1]
    past = n_pages * page
    mp, ms = b * s, bd * t_new
    group = N_HEADS // N_KV_HEADS

    tm_p = min(256, s)
    tm_s = min(256, ms)
    tq = min(256, s)
    tm_peer = min(512, ms, mp)
    te = 2048

    w1 = _pack_w_in(w_in, d, dkv, dqi)
    wo = w_o_attn.astype(BF16)
    wc = w_conv_out.astype(BF16)
    wout = w_out.astype(BF16)
    wq = w_q_peer.astype(BF16)
    sk1 = sub_keys1.astype(BF16)
    sk2 = sub_keys2.astype(BF16)
    u = u_emb.astype(BF16)
    vt = v_emb.astype(BF16).T
    g1 = g_norm1.reshape(1, d)
    g2 = g_norm2.reshape(1, d)
    gq = g_q.reshape(1, dh)
    gk = g_k.reshape(1, dh)

    mod = _ada(jnp.concatenate([c_prompt, c_sample], axis=0), w_ada, b_ada)
    mod_p = mod[:b].reshape(b, 1, N_ADA * d)
    mod_s = jnp.repeat(mod[b:], t_new, axis=0)

    xp = x_prompt.reshape(mp, d)
    q, k, kb, v, vb, qi, kiwi, kib, z, ga, gb = _inproj(xp, mod_p, g1, gq, gk, w1, tm=tm_p,
                                                        tiles_per_group=s // tm_p)
    wit = kiwi[:, IDX_DIM:IDX_DIM + N_IDX_HEADS].T
    vbt = jnp.swapaxes(vb.reshape(b, s // tq, tq, dkv), 2, 3)
    attn = _pattn(qi, kib, wit, q, kb, vbt, batch=b, seq=s, tq=tq, topk=min(TOPK_MAX, s // 4))
    cv = _pconv(z, w_dw, b_dw, ln_g, ln_b, batch=b, seq=s, tm=tm_p)
    x1, h2, qp = _mid(xp, attn, cv, ga, gb, mod_p, g2, wo, wc, wout, wq, tm=tm_p, tiles_per_group=s // tm_p)
    y_prompt = _peer(h2, qp, sk1, sk2, u, vt, x1, mod_p, tm=tm_peer, te=te, tiles_per_group=s // tm_peer)
    k_prompt = k.reshape(b, s, N_KV_HEADS, dh)
    v_prompt = v.reshape(b, s, N_KV_HEADS, dh)
    idx_k_prompt = kiwi[:, :IDX_DIM].reshape(b, s, IDX_DIM)
    conv_prompt = z.reshape(b, s, d)[:, s - (CONV_WIDTH - 1):]

    xs = x_sample.reshape(ms, d)
    q, k, kb, v, vb, qi, kiwi, kib, z, ga, gb = _inproj(xs, mod_s, g1, gq, gk, w1, tm=tm_s, tiles_per_group=1)
    lp = -(-(past + t_new) // LANES) * LANES
    qi32 = qi.reshape(bd, t_new * N_IDX_HEADS, IDX_DIM)
    w32 = kiwi[:, IDX_DIM:IDX_DIM + N_IDX_HEADS].reshape(bd, t_new * N_IDX_HEADS, 1)
    kib_new_t = jnp.swapaxes(kib[:, :IDX_DIM].reshape(bd, t_new, IDX_DIM), 1, 2)
    keys = _sidx(page_table, qi32, w32, kib_new_t, jnp.swapaxes(cache_idx_k, 1, 2), lp=lp)
    keys_t = keys.reshape(ms, lp).T
    thr, jj = _sthr(keys_t, tq=min(256, ms), kc=LANES, topk=min(TOPK_MAX, (past + t_new) // 4))
    rep = lambda a: jnp.repeat(a.reshape(bd, t_new, -1), group, axis=1)
    q16 = q.reshape(bd, t_new, N_KV_HEADS, group, dh).transpose(0, 2, 1, 3, 4).reshape(bd, N_KV_HEADS,
                                                                                      t_new * group, dh)
    tpos16 = jnp.repeat(past + jnp.arange(t_new, dtype=I32), group).reshape(t_new * group, 1)
    slopes = jnp.exp2(-8.0 * jnp.arange(1, N_HEADS + 1, dtype=F32) / N_HEADS).reshape(N_KV_HEADS, 1, group)
    slope16 = jnp.broadcast_to(slopes, (N_KV_HEADS, t_new, group)).reshape(N_KV_HEADS, t_new * group, 1)
    o16 = _sattn(page_table, q16, rep(keys), rep(thr.reshape(ms, 1)), rep(jj.reshape(ms, 1)), tpos16, slope16,
                 kb.reshape(bd, t_new, dkv), vb.reshape(bd, t_new, dkv),
                 cache_k.reshape(-1, page * N_KV_HEADS, dh), cache_v.reshape(-1, page * N_KV_HEADS, dh), lp=lp)
    attn = o16.reshape(bd, N_KV_HEADS, t_new, group, dh).transpose(0, 2, 1, 3, 4).reshape(ms, d)
    cv3, conv_sample = _sconv(z.reshape(bd, t_new, d), state_conv, w_dw, b_dw, ln_g, ln_b, bs=min(8, bd))
    x1, h2, qp = _mid(xs, attn, cv3.reshape(ms, d), ga, gb, mod_s, g2, wo, wc, wout, wq, tm=tm_s,
                      tiles_per_group=1)
    y_sample = _peer(h2, qp, sk1, sk2, u, vt, x1, mod_s, tm=min(tm_peer, ms), te=te, tiles_per_group=1)

    return (y_prompt.reshape(b, s, d), y_sample.reshape(bd, t_new, d), k_prompt, v_prompt, idx_k_prompt,
            conv_prompt, k.reshape(bd, t_new, N_KV_HEADS, dh), v.reshape(bd, t_new, N_KV_HEADS, dh),
            kiwi[:, :IDX_DIM].reshape(bd, t_new, IDX_DIM), conv_sample)
```
